```python
import math
import jax
import jax.numpy as jnp
from jax import lax
import numpy as np

D_MODEL = 1024
BATCH = 16
SEQ = 2048
DEPTH = 4
DEC_BATCH = 128
DEC_SEQ = 4
PAST_LEN = 8192
PAGE_SIZE = 128

MIX_WIDTH = D_MODEL
GROUP_WIDTH = MIX_WIDTH // 4
HEAD_DIM = 64
MLA_HEADS = GROUP_WIDTH // HEAD_DIM
MLA_Q_RANK = D_MODEL // 4
MLA_KV_RANK = D_MODEL // 8
MLA_NOPE = HEAD_DIM
MLA_ROPE = HEAD_DIM // 2
MLA_V = HEAD_DIM
MLA_SCALE = 1.0 / math.sqrt(MLA_NOPE + MLA_ROPE)
ATTN_QBLOCK = 128
POOL_WINDOWS = (2, 4, 8, 16)
POOL_GROUPS = 4
POOL_GROUP = GROUP_WIDTH // POOL_GROUPS
POOL_HIST = 15
SGU_CHUNK = 128
SGU_HEADS = GROUP_WIDTH // HEAD_DIM
MOBA_HEADS = GROUP_WIDTH // HEAD_DIM
MOBA_BLOCK = 256
MOBA_TOPK = 3
MOBA_QCHUNK = 16
MOBA_SCALE = 1.0 / math.sqrt(HEAD_DIM)
D_FF = 256 * ((8 * D_MODEL // 3 + 255) // 256)
PLE_DIM = 256
ROPE_THETA = 10000.0
EPS = 1e-6
IN_WIDTHS = (MLA_Q_RANK, MLA_KV_RANK, MLA_ROPE, GROUP_WIDTH, GROUP_WIDTH, GROUP_WIDTH,
             GROUP_WIDTH, GROUP_WIDTH, GROUP_WIDTH)
IN_COLS = 1952

kernel_name = 'hymba_mla_pool_sgu_moba_macaron_step'


def in_offsets():
    return [int(o) for o in np.cumsum(IN_WIDTHS)[:-1]]


def rms_norm(x, g):
    xf = x.astype(jnp.float32)
    y = xf * lax.rsqrt(jnp.mean(xf * xf, axis=-1, keepdims=True) + EPS)
    return (y * g.astype(jnp.float32)).astype(x.dtype)


def rope(x, pos):
    half = x.shape[-1] // 2
    inv_freq = ROPE_THETA ** (-jnp.arange(half, dtype=jnp.float32) / half)
    ang = pos.astype(jnp.float32)[:, None] * inv_freq[None, :]
    shape = (1, x.shape[1]) + (1,) * (x.ndim - 3) + (half,)
    cos = jnp.cos(ang).reshape(shape)
    sin = jnp.sin(ang).reshape(shape)
    x1 = x[..., :half].astype(jnp.float32)
    x2 = x[..., half:].astype(jnp.float32)
    return jnp.concatenate([x1 * cos - x2 * sin, x2 * cos + x1 * sin], axis=-1).astype(x.dtype)


def swiglu(x, w_gate, w_up, w_down):
    return (jax.nn.silu(x @ w_gate) * (x @ w_up)) @ w_down


def mla_attention(q_nope, q_rope, k_nope, k_rope, lat, w_uv, q_pos):
    b, tq, h, _ = q_nope.shape
    k_pos = jnp.arange(lat.shape[1])
    qb = ATTN_QBLOCK if tq % ATTN_QBLOCK == 0 else tq
    nq = tq // qb

    def block(args):
        qn, qr, qp = args
        s = jnp.einsum('bqhd,bkhd->bhqk', qn, k_nope) + jnp.einsum('bqhd,bkd->bhqk', qr, k_rope)
        s = s.astype(jnp.float32) * MLA_SCALE
        s = jnp.where(k_pos[None, None, None, :] <= qp[None, None, :, None], s, -jnp.inf)
        p = jax.nn.softmax(s, axis=-1).astype(lat.dtype)
        o_lat = jnp.einsum('bhqk,bkr->bqhr', p, lat)
        return jnp.einsum('bqhr,rhd->bqhd', o_lat, w_uv)

    def split(a):
        return a.reshape((b, nq, qb) + a.shape[2:]).swapaxes(0, 1)

    out = lax.map(block, (split(q_nope), split(q_rope), q_pos.reshape(nq, qb)))
    return out.swapaxes(0, 1).reshape(b, tq, h, MLA_V)


def pool_mixer(xb, prev, pos, w_pool, scale):
    b, t, c = xb.shape
    full = jnp.concatenate([prev, xb], axis=1)
    cs = jnp.cumsum(full.astype(jnp.float32), axis=1)
    cs = jnp.concatenate([jnp.zeros((b, 1, c), jnp.float32), cs], axis=1)
    end = cs[:, POOL_HIST + 1:]
    means = []
    for g, w in enumerate(POOL_WINDOWS):
        sl = slice(g * POOL_GROUP, (g + 1) * POOL_GROUP)
        start = cs[:, POOL_HIST + 1 - w:POOL_HIST + 1 - w + t, sl]
        count = jnp.minimum(pos + 1, w).astype(jnp.float32)[None, :, None]
        means.append((end[:, :, sl] - start) / count)
    pooled = jnp.concatenate(means, axis=-1) - xb.astype(jnp.float32)
    pooled = pooled.astype(xb.dtype).reshape(b, t, POOL_GROUPS, POOL_GROUP)
    y = jnp.einsum('btgc,gcd->btgd', pooled, w_pool).reshape(b, t, c) * scale
    return y, full[:, -POOL_HIST:]


def sgu_mixer(u, v, w_s, b_s):
    b, t, c = v.shape
    L = SGU_CHUNK if t % SGU_CHUNK == 0 else t
    vh = v.reshape(b, t // L, L, SGU_HEADS, HEAD_DIM)
    w = w_s[:, :L, :L] * jnp.tril(jnp.ones((L, L), w_s.dtype))
    mixed = jnp.einsum('hij,bnjhd->bnihd', w, vh) + b_s[:, :L].T[None, None, :, :, None]
    return u * mixed.reshape(b, t, c)


def moba_attention(q, k_parts, v_parts, pos0):
    b, tq, h, dh = q.shape
    tk = sum(p.shape[1] for p in k_parts)
    n_blk = -(-tk // MOBA_BLOCK)
    pad = n_blk * MOBA_BLOCK - tk
    zeros = jnp.zeros((b, pad, h, dh), q.dtype)
    kb = jnp.concatenate(list(k_parts) + [zeros], axis=1).reshape(b, n_blk, MOBA_BLOCK, h, dh)
    vb = jnp.concatenate(list(v_parts) + [zeros], axis=1).reshape(b, n_blk, MOBA_BLOCK, h, dh)
    n_cand = (pos0 + tq - 1) // MOBA_BLOCK
    k_sel = min(MOBA_TOPK, n_cand)
    if k_sel > 0:
        means = jnp.mean(kb[:, :n_cand], axis=2, dtype=jnp.float32).astype(q.dtype)
    bi = jnp.arange(b)[:, None, None, None]
    hi = jnp.arange(h)[None, :, None, None]
    offs = jnp.arange(MOBA_BLOCK)
    qc = MOBA_QCHUNK if tq % MOBA_QCHUNK == 0 else tq
    nc = tq // qc

    def chunk(args):
        qq, qp = args
        own = jnp.broadcast_to((qp // MOBA_BLOCK)[None, None, :, None], (b, h, qc, 1))
        if k_sel > 0:
            s = jnp.einsum('bqhd,bjhd->bhqj', qq, means).astype(jnp.float32)
            s = jnp.where(jnp.arange(n_cand)[None, None, None, :] < own, s, -jnp.inf)
            _, top = lax.top_k(s, k_sel)
            idx = jnp.concatenate([top, own], axis=-1)
            valid = jnp.concatenate([top < own, jnp.ones_like(own, dtype=bool)], axis=-1)
        else:
            idx = own
            valid = jnp.ones_like(own, dtype=bool)
        kg = kb[bi, idx, :, hi, :]
        vg = vb[bi, idx, :, hi, :]
        s = jnp.einsum('bqhd,bhqskd->bhqsk', qq, kg).astype(jnp.float32) * MOBA_SCALE
        kpos = idx[..., None] * MOBA_BLOCK + offs
        mask = valid[..., None] & (kpos <= qp[None, None, :, None, None])
        s = jnp.where(mask, s, -jnp.inf)
        n_s = idx.shape[-1]
        p = jax.nn.softmax(s.reshape(b, h, qc, n_s * MOBA_BLOCK), axis=-1)
        p = p.reshape(b, h, qc, n_s, MOBA_BLOCK).astype(vg.dtype)
        return jnp.einsum('bhqsk,bhqskd->bqhd', p, vg)

    qs = q.reshape(b, nc, qc, h, dh).swapaxes(0, 1)
    q_pos = (pos0 + jnp.arange(tq)).reshape(nc, qc)
    out = lax.map(chunk, (qs, q_pos))
    return out.swapaxes(0, 1).reshape(b, tq, h, dh)


def layer_forward(x, pe, pos0, past, lp):
    b, t, _ = x.shape
    pos = pos0 + jnp.arange(t)
    x = x + 0.5 * swiglu(rms_norm(x, lp['ffn1_norm']), lp['ffn1_w_gate'], lp['ffn1_w_up'], lp['ffn1_w_down'])
    hn = rms_norm(x, lp['mix_norm'])
    z = hn @ lp['w_in']
    cq, ckv, kr, zb, zu, zv, zq, zk, zvd = jnp.split(z, in_offsets(), axis=-1)

    cq = rms_norm(cq, lp['mla_q_norm'])
    qa = jnp.einsum('btr,rhd->bthd', cq, lp['mla_w_uq'])
    qa_nope = rms_norm(qa[..., :MLA_NOPE], lp['mla_q_nope_norm'])
    qa_rope = rope(rms_norm(qa[..., MLA_NOPE:], lp['mla_q_rope_norm']), pos)
    lat_new = rms_norm(ckv, lp['mla_kv_norm'])
    kr_new = rope(rms_norm(kr, lp['mla_k_rope_norm']), pos)
    if past is None:
        lat_all, kr_all = lat_new, kr_new
    else:
        lat_all = jnp.concatenate([past[0], lat_new], axis=1)
        kr_all = jnp.concatenate([past[1], kr_new], axis=1)
    ka_nope = rms_norm(jnp.einsum('btr,rhd->bthd', lat_all, lp['mla_w_uk']), lp['mla_k_nope_norm'])
    out_a = mla_attention(qa_nope, qa_rope, ka_nope, kr_all, lat_all, lp['mla_w_uv'], pos)

    prev = jnp.zeros((b, POOL_HIST, GROUP_WIDTH), zb.dtype) if past is None else past[4]
    out_b, pool_state = pool_mixer(zb, prev, pos, lp['pool_w'], lp['pool_scale'])

    zu = jax.nn.gelu(zu)
    zv = jax.nn.gelu(zv)
    out_c = sgu_mixer(zu, zv, lp['sgu_w'], lp['sgu_b'])

    qd = rope(rms_norm(zq.reshape(b, t, MOBA_HEADS, HEAD_DIM), lp['moba_q_norm']), pos)
    kd_new = rope(rms_norm(zk.reshape(b, t, MOBA_HEADS, HEAD_DIM), lp['moba_k_norm']), pos)
    vd_new = zvd.reshape(b, t, MOBA_HEADS, HEAD_DIM)
    if past is None:
        out_d = moba_attention(qd, [kd_new], [vd_new], pos0)
    else:
        out_d = moba_attention(qd, [past[2], kd_new], [past[3], vd_new], pos0)

    mixed = jnp.concatenate([out_a.reshape(b, t, -1), out_b, out_c, out_d.reshape(b, t, -1)], axis=-1)
    x = x + mixed @ lp['w_out']
    x = x + 0.5 * swiglu(rms_norm(x, lp['ffn2_norm']), lp['ffn2_w_gate'], lp['ffn2_w_up'], lp['ffn2_w_down'])
    gate = jax.nn.sigmoid(rms_norm(x, lp['ple_gate_norm']) @ lp['ple_w_gate'])
    x = x + gate * (pe @ lp['ple_w'])
    return x, (lat_new, kr_new, kd_new, vd_new, pool_state, zv)


def setup_inputs(seed: int = 0) -> dict:
    key = jax.random.key(seed)
    keys = list(jax.random.split(key, 64))
    f32 = jnp.float32

    def nrm(shape, scale):
        return jax.random.normal(keys.pop(), shape, f32) * scale

    def gain(shape):
        return 1.0 + 0.02 * jax.random.normal(keys.pop(), shape, f32)

    n_pages = PAST_LEN // PAGE_SIZE
    n_pool = (DEC_BATCH * n_pages * 5) // 4
    perm = jax.random.permutation(keys.pop(), n_pool)
    page_table = perm[:DEC_BATCH * n_pages].reshape(DEC_BATCH, n_pages).astype(jnp.int32)
    L = DEPTH
    return {
        'x_prompt': jax.random.normal(keys.pop(), (BATCH, SEQ, D_MODEL), f32),
        'x_sample': jax.random.normal(keys.pop(), (DEC_BATCH, DEC_SEQ, D_MODEL), f32),
        'cache_mla_latent': jax.random.normal(keys.pop(), (L, n_pool, PAGE_SIZE, MLA_KV_RANK), f32),
        'cache_mla_krope': jax.random.normal(keys.pop(), (L, n_pool, PAGE_SIZE, MLA_ROPE), f32),
        'cache_moba_k': jax.random.normal(keys.pop(), (L, n_pool, PAGE_SIZE, MOBA_HEADS, HEAD_DIM), f32),
        'cache_moba_v': jax.random.normal(keys.pop(), (L, n_pool, PAGE_SIZE, MOBA_HEADS, HEAD_DIM), f32),
        'state_pool': jax.random.normal(keys.pop(), (L, DEC_BATCH, POOL_HIST, GROUP_WIDTH), f32),
        'page_table': page_table,
        'p_prompt': jax.random.normal(keys.pop(), (L, BATCH, SEQ, PLE_DIM), f32),
        'p_sample': jax.random.normal(keys.pop(), (L, DEC_BATCH, DEC_SEQ, PLE_DIM), f32),
        'ffn1_norm': gain((L, D_MODEL)),
        'ffn1_w_gate': nrm((L, D_MODEL, D_FF), D_MODEL ** -0.5),
        'ffn1_w_up': nrm((L, D_MODEL, D_FF), D_MODEL ** -0.5),
        'ffn1_w_down': nrm((L, D_FF, D_MODEL), D_FF ** -0.5),
        'mix_norm': gain((L, D_MODEL)),
        'w_in': nrm((L, D_MODEL, IN_COLS), D_MODEL ** -0.5),
        'mla_q_norm': gain((L, MLA_Q_RANK)),
        'mla_w_uq': nrm((L, MLA_Q_RANK, MLA_HEADS, MLA_NOPE + MLA_ROPE), MLA_Q_RANK ** -0.5),
        'mla_kv_norm': gain((L, MLA_KV_RANK)),
        'mla_w_uk': nrm((L, MLA_KV_RANK, MLA_HEADS, MLA_NOPE), MLA_KV_RANK ** -0.5),
        'mla_w_uv': nrm((L, MLA_KV_RANK, MLA_HEADS, MLA_V), MLA_KV_RANK ** -0.5),
        'mla_q_nope_norm': gain((L, MLA_NOPE)),
        'mla_q_rope_norm': gain((L, MLA_ROPE)),
        'mla_k_nope_norm': gain((L, MLA_NOPE)),
        'mla_k_rope_norm': gain((L, MLA_ROPE)),
        'pool_w': nrm((L, POOL_GROUPS, POOL_GROUP, POOL_GROUP), POOL_GROUP ** -0.5),
        'pool_scale': gain((L, GROUP_WIDTH)),
        'sgu_w': nrm((L, SGU_HEADS, SGU_CHUNK, SGU_CHUNK), SGU_CHUNK ** -0.5),
        'sgu_b': gain((L, SGU_HEADS, SGU_CHUNK)),
        'moba_q_norm': gain((L, HEAD_DIM)),
        'moba_k_norm': gain((L, HEAD_DIM)),
        'w_out': nrm((L, MIX_WIDTH, D_MODEL), MIX_WIDTH ** -0.5),
        'ffn2_norm': gain((L, D_MODEL)),
        'ffn2_w_gate': nrm((L, D_MODEL, D_FF), D_MODEL ** -0.5),
        'ffn2_w_up': nrm((L, D_MODEL, D_FF), D_MODEL ** -0.5),
        'ffn2_w_down': nrm((L, D_FF, D_MODEL), D_FF ** -0.5),
        'ple_w': nrm((L, PLE_DIM, D_MODEL), PLE_DIM ** -0.5),
        'ple_gate_norm': gain((L, D_MODEL)),
        'ple_w_gate': nrm((L, D_MODEL, D_MODEL), D_MODEL ** -0.5),
    }


def reference(x_prompt, x_sample, cache_mla_latent, cache_mla_krope, cache_moba_k, cache_moba_v,
              state_pool, page_table, p_prompt, p_sample,
              ffn1_norm, ffn1_w_gate, ffn1_w_up, ffn1_w_down, mix_norm, w_in,
              mla_q_norm, mla_w_uq, mla_kv_norm, mla_w_uk, mla_w_uv,
              mla_q_nope_norm, mla_q_rope_norm, mla_k_nope_norm, mla_k_rope_norm,
              pool_w, pool_scale, sgu_w, sgu_b, moba_q_norm, moba_k_norm, w_out,
              ffn2_norm, ffn2_w_gate, ffn2_w_up, ffn2_w_down,
              ple_w, ple_gate_norm, ple_w_gate):
    db, n_pages = page_table.shape
    past_len = n_pages * PAGE_SIZE
    y_prompt, y_sample = x_prompt, x_sample
    st_p = [[] for _ in range(5)]
    st_s = [[] for _ in range(6)]
    for i in range(DEPTH):
        lp = {
            'ffn1_norm': ffn1_norm[i], 'ffn1_w_gate': ffn1_w_gate[i], 'ffn1_w_up': ffn1_w_up[i],
            'ffn1_w_down': ffn1_w_down[i], 'mix_norm': mix_norm[i], 'w_in': w_in[i],
            'mla_q_norm': mla_q_norm[i], 'mla_w_uq': mla_w_uq[i], 'mla_kv_norm': mla_kv_norm[i],
            'mla_w_uk': mla_w_uk[i], 'mla_w_uv': mla_w_uv[i],
            'mla_q_nope_norm': mla_q_nope_norm[i], 'mla_q_rope_norm': mla_q_rope_norm[i],
            'mla_k_nope_norm': mla_k_nope_norm[i], 'mla_k_rope_norm': mla_k_rope_norm[i],
            'pool_w': pool_w[i], 'pool_scale': pool_scale[i], 'sgu_w': sgu_w[i], 'sgu_b': sgu_b[i],
            'moba_q_norm': moba_q_norm[i], 'moba_k_norm': moba_k_norm[i], 'w_out': w_out[i],
            'ffn2_norm': ffn2_norm[i], 'ffn2_w_gate': ffn2_w_gate[i], 'ffn2_w_up': ffn2_w_up[i],
            'ffn2_w_down': ffn2_w_down[i], 'ple_w': ple_w[i], 'ple_gate_norm': ple_gate_norm[i],
            'ple_w_gate': ple_w_gate[i],
        }
        y_prompt, sp = layer_forward(y_prompt, p_prompt[i], 0, None, lp)
        past = (
            cache_mla_latent[i, page_table].reshape(db, past_len, MLA_KV_RANK),
            cache_mla_krope[i, page_table].reshape(db, past_len, MLA_ROPE),
            cache_moba_k[i, page_table].reshape(db, past_len, MOBA_HEADS, HEAD_DIM),
            cache_moba_v[i, page_table].reshape(db, past_len, MOBA_HEADS, HEAD_DIM),
            state_pool[i],
        )
        y_sample, ss = layer_forward(y_sample, p_sample[i], past_len, past, lp)
        for j in range(5):
            st_p[j].append(sp[j])
        for j in range(6):
            st_s[j].append(ss[j])
    new_mla_latent_prompt = jnp.stack(st_p[0])
    new_mla_krope_prompt = jnp.stack(st_p[1])
    new_moba_k_prompt = jnp.stack(st_p[2])
    new_moba_v_prompt = jnp.stack(st_p[3])
    new_pool_state_prompt = jnp.stack(st_p[4])
    new_mla_latent_sample = jnp.stack(st_s[0])
    new_mla_krope_sample = jnp.stack(st_s[1])
    new_moba_k_sample = jnp.stack(st_s[2])
    new_moba_v_sample = jnp.stack(st_s[3])
    new_pool_state_sample = jnp.stack(st_s[4])
    new_sgu_v_sample = jnp.stack(st_s[5])
    return (y_prompt, y_sample, new_mla_latent_prompt, new_mla_krope_prompt, new_mla_latent_sample,
            new_mla_krope_sample, new_moba_k_prompt, new_moba_v_prompt, new_moba_k_sample,
            new_moba_v_sample, new_pool_state_prompt, new_pool_state_sample, new_sgu_v_sample)
```

```python
import functools
import math

import numpy as np
import jax
import jax.numpy as jnp
from jax import lax
from jax.experimental import pallas as pl
from jax.experimental.pallas import tpu as pltpu

F32 = jnp.float32
BF = jnp.bfloat16

EPS = 1e-6
ROPE_THETA = 10000.0
HEAD_DIM = 64
N_HEADS = 4
GROUP_WIDTH = N_HEADS * HEAD_DIM
MLA_NOPE = HEAD_DIM
MLA_ROPE = HEAD_DIM // 2
MLA_HEAD_PAD = 128
MLA_SCALE = 1.0 / math.sqrt(MLA_NOPE + MLA_ROPE)
POOL_WINDOWS = (2, 4, 8, 16)
POOL_HIST = 15
SGU_CHUNK = 128
MOBA_BLOCK = 256
MOBA_TOPK = 3
MOBA_SCALE = 1.0 / math.sqrt(HEAD_DIM)
NEG = -1e30
LANES = 128
Q_PAD = 8
VMEM_LIMIT = 56 * 1024 * 1024


def _dot(a, b):
    return jnp.dot(a, b, preferred_element_type=F32)


def _dot_nt(a, b):
    return lax.dot_general(a, b, (((1,), (1,)), ((), ())), preferred_element_type=F32)


def _rms(x, g):
    ms = jnp.mean(x * x, axis=-1, keepdims=True)
    return x * lax.rsqrt(ms + EPS) * g


def _group_rms(x, gmat_ref, gain):
    ms = _dot((x * x).astype(BF), gmat_ref[...])
    return x * lax.rsqrt(ms + EPS) * gain


def _rope(x, cos, sin_a, sin_b, half):
    w = x.shape[-1]
    return x * cos + pltpu.roll(x, w - half, 1) * sin_a + pltpu.roll(x, half, 1) * sin_b


def _lane_group(shape, width):
    lane = lax.broadcasted_iota(jnp.int32, shape, len(shape) - 1)
    return lane // width


def _tile_lanes(t, n):
    return t if n == 1 else jnp.concatenate([t] * n, axis=-1)


def _params(sem, vmem=VMEM_LIMIT):
    return pltpu.CompilerParams(dimension_semantics=sem, vmem_limit_bytes=vmem)


def _row_tile(rows, want):
    tm = min(rows, want)
    assert rows % tm == 0, (rows, tm)
    return tm


def _ffn_kernel(x_ref, g_ref, wg_ref, wu_ref, wd_ref, o_ref, hn_ref, acc_ref, *, nf):
    f = pl.program_id(1)

    @pl.when(f == 0)
    def _():
        hn_ref[...] = _rms(x_ref[...], g_ref[...]).astype(BF)
        acc_ref[...] = jnp.zeros_like(acc_ref)

    hn = hn_ref[...]
    gate = _dot(hn, wg_ref[...])
    up = _dot(hn, wu_ref[...])
    h = (gate * jax.nn.sigmoid(gate)) * up
    acc_ref[...] += _dot(h.astype(BF), wd_ref[...])

    @pl.when(f == nf - 1)
    def _():
        o_ref[...] = x_ref[...] + 0.5 * acc_ref[...]


def _ffn(x, layer, g, wg, wu, wd):
    rows, d = x.shape
    dff = wg.shape[-1]
    tm = _row_tile(rows, 512)
    tf = dff // 2 if (dff // 2) % LANES == 0 else dff
    nf = dff // tf
    return pl.pallas_call(
        functools.partial(_ffn_kernel, nf=nf),
        out_shape=jax.ShapeDtypeStruct((rows, d), F32),
        grid=(rows // tm, nf),
        in_specs=[
            pl.BlockSpec((tm, d), lambda r, f: (r, 0)),
            pl.BlockSpec((None, 1, d), lambda r, f: (layer, 0, 0)),
            pl.BlockSpec((None, d, tf), lambda r, f: (layer, 0, f)),
            pl.BlockSpec((None, d, tf), lambda r, f: (layer, 0, f)),
            pl.BlockSpec((None, tf, d), lambda r, f: (layer, f, 0)),
        ],
        out_specs=pl.BlockSpec((tm, d), lambda r, f: (r, 0)),
        scratch_shapes=[pltpu.VMEM((tm, d), BF), pltpu.VMEM((tm, d), F32)],
        compiler_params=_params(("parallel", "arbitrary")),
        name="ffn",
    )(x, g, wg, wu, wd)


_C_CQ, _C_CKV, _C_KR, _C_ZB, _C_ZU, _C_ZV, _C_ZQ, _C_ZK, _C_ZVD, _C_END = (
    0, 256, 384, 512, 768, 1024, 1280, 1536, 1792, 2048)
_KR_LANE = 64


def _inproj_kernel(x_ref, g_ref, w_ref, qn_ref, wuq_ref, gq_ref, kvn_ref, wuk_ref, gk_ref,
                   gkr_ref, gm512_ref, gm256_ref, dqn_ref, dkn_ref, rt_ref,
                   qmla_ref, kmla_ref, lat_ref, latb_ref, kr_ref, zb_ref, u_ref, v_ref,
                   qd_ref, kd_ref, kdb_ref, vd_ref, vdb_ref):
    hn = _rms(x_ref[...], g_ref[...]).astype(BF)
    z = _dot(hn, w_ref[...])

    rt = rt_ref[...]
    cos_m, sa_m, sb_m = rt[:, 0:128], rt[:, 128:256], rt[:, 256:384]
    cos_d, sa_d, sb_d = rt[:, 384:512], rt[:, 512:640], rt[:, 640:768]

    cq = _rms(z[:, _C_CQ:_C_CKV], qn_ref[...])
    qa = _dot(cq.astype(BF), wuq_ref[...])
    qa = _group_rms(qa, gm512_ref, gq_ref[...])
    qa = _rope(qa, _tile_lanes(cos_m, N_HEADS), _tile_lanes(sa_m, N_HEADS),
               _tile_lanes(sb_m, N_HEADS), MLA_ROPE // 2)
    qmla_ref[...] = qa.astype(BF)

    lat = _rms(z[:, _C_CKV:_C_KR], kvn_ref[...])
    lat_ref[...] = lat
    lat_b = lat.astype(BF)
    latb_ref[...] = lat_b

    kr_raw = z[:, _C_KR:_C_ZB]
    ms = jnp.sum(kr_raw * kr_raw, axis=-1, keepdims=True) * (1.0 / MLA_ROPE)
    kr = kr_raw * lax.rsqrt(ms + EPS) * gkr_ref[...]
    kr = _rope(kr, cos_m, sa_m, sb_m, MLA_ROPE // 2)
    kr_ref[...] = kr[:, _KR_LANE:_KR_LANE + MLA_ROPE]

    kn = _group_rms(_dot(lat_b, wuk_ref[...]), gm512_ref, gk_ref[...])
    kmla_ref[...] = (kn + _tile_lanes(kr, N_HEADS)).astype(BF)

    zb_ref[...] = z[:, _C_ZB:_C_ZU]
    u_ref[...] = jax.nn.gelu(z[:, _C_ZU:_C_ZV])
    v_ref[...] = jax.nn.gelu(z[:, _C_ZV:_C_ZQ])

    qd = _group_rms(z[:, _C_ZQ:_C_ZK], gm256_ref, dqn_ref[...])
    qd = _rope(qd, _tile_lanes(cos_d, 2), _tile_lanes(sa_d, 2), _tile_lanes(sb_d, 2), HEAD_DIM // 2)
    qd_ref[...] = (qd * MOBA_SCALE).astype(BF)
    kd = _group_rms(z[:, _C_ZK:_C_ZVD], gm256_ref, dkn_ref[...])
    kd = _rope(kd, _tile_lanes(cos_d, 2), _tile_lanes(sa_d, 2), _tile_lanes(sb_d, 2), HEAD_DIM // 2)
    kd_ref[...] = kd
    kdb_ref[...] = kd.astype(BF)
    vd = z[:, _C_ZVD:_C_END]
    vd_ref[...] = vd
    vdb_ref[...] = vd.astype(BF)


def _inproj(x, layer, wp, rope_tab):
    rows, d = x.shape
    tm = _row_tile(rows, 512)
    n_tab = rope_tab.shape[0] // tm
    assert rope_tab.shape[0] % tm == 0

    def wspec(a):
        shape = a.shape[1:]
        return pl.BlockSpec((None,) + shape, lambda r: (layer,) + (0,) * len(shape))

    def cspec(a):
        return pl.BlockSpec(a.shape, lambda r: (0,) * a.ndim)

    def ospec(w):
        return pl.BlockSpec((tm, w), lambda r: (r, 0))

    outs = [(4 * MLA_HEAD_PAD, BF), (4 * MLA_HEAD_PAD, BF), (128, F32), (128, BF), (MLA_ROPE, F32),
            (256, F32), (256, F32), (256, F32), (256, BF), (256, F32), (256, BF), (256, F32), (256, BF)]
    return pl.pallas_call(
        _inproj_kernel,
        out_shape=[jax.ShapeDtypeStruct((rows, w), dt) for w, dt in outs],
        grid=(rows // tm,),
        in_specs=[
            pl.BlockSpec((tm, d), lambda r: (r, 0)),
            wspec(wp['mix_norm']), wspec(wp['w_in']), wspec(wp['mla_q_norm']), wspec(wp['w_uq']),
            wspec(wp['gq']), wspec(wp['mla_kv_norm']), wspec(wp['w_uk']), wspec(wp['gk']),
            wspec(wp['gkr']), cspec(wp['gm512']), cspec(wp['gm256']), wspec(wp['dqn']), wspec(wp['dkn']),
            pl.BlockSpec((tm, rope_tab.shape[1]), lambda r: (r % n_tab, 0)),
        ],
        out_specs=[ospec(w) for w, _ in outs],
        compiler_params=_params(("parallel",)),
        name="inproj",
    )(x, wp['mix_norm'], wp['w_in'], wp['mla_q_norm'], wp['w_uq'], wp['gq'], wp['mla_kv_norm'],
      wp['w_uk'], wp['gk'], wp['gkr'], wp['gm512'], wp['gm256'], wp['dqn'], wp['dkn'], rope_tab)


def _pool_kernel(x_ref, w_ref, sc_ref, o_ref, *, seq, pos_start):
    x = x_ref[...]
    rows, c = x.shape
    t = lax.broadcasted_iota(jnp.int32, (rows // seq, seq, c), 1).reshape(rows, c)

    def shifted(a, k):
        return jnp.where(t >= k, pltpu.roll(a, k, 0), 0.0)

    s2 = x + shifted(x, 1)
    s4 = s2 + shifted(s2, 2)
    s8 = s4 + shifted(s4, 4)
    s16 = s8 + shifted(s8, 8)
    grp = _lane_group((rows, c), c // len(POOL_WINDOWS))
    win = jnp.where(grp == 0, s2, jnp.where(grp == 1, s4, jnp.where(grp == 2, s8, s16)))
    wsize = jnp.where(grp == 0, POOL_WINDOWS[0], jnp.where(grp == 1, POOL_WINDOWS[1],
                      jnp.where(grp == 2, POOL_WINDOWS[2], POOL_WINDOWS[3])))
    count = jnp.maximum(jnp.minimum(t + (pos_start + 1), wsize), 1).astype(F32)
    pooled = win / count - x
    o_ref[...] = _dot(pooled.astype(BF), w_ref[...]) * sc_ref[...]


def _pool(zb, layer, w_bd, scale, *, seq, n_seq_blk, pos_start):
    rows, c = zb.shape
    tm = seq * n_seq_blk
    assert rows % tm == 0
    return pl.pallas_call(
        functools.partial(_pool_kernel, seq=seq, pos_start=pos_start),
        out_shape=jax.ShapeDtypeStruct((rows, c), F32),
        grid=(rows // tm,),
        in_specs=[
            pl.BlockSpec((tm, c), lambda r: (r, 0)),
            pl.BlockSpec((None, c, c), lambda r: (layer, 0, 0)),
            pl.BlockSpec((None, 1, c), lambda r: (layer, 0, 0)),
        ],
        out_specs=pl.BlockSpec((tm, c), lambda r: (r, 0)),
        compiler_params=_params(("parallel",)),
        name="pool",
    )(zb, w_bd, scale)


def _sgu_kernel(u_ref, v_ref, w_ref, b_ref, o_ref, *, mix_len, chunk):
    tm, c = u_ref.shape
    r = lax.broadcasted_iota(jnp.int32, (chunk, chunk), 0)
    col = lax.broadcasted_iota(jnp.int32, (chunk, chunk), 1)
    keep = (r // mix_len == col // mix_len) & (col <= r)
    grp = _lane_group((chunk, c), HEAD_DIM)
    w_heads = [jnp.where(keep, w_ref[h], 0.0).astype(BF) for h in range(N_HEADS)]
    bias = b_ref[...]
    for ch in range(tm // chunk):
        rs = slice(ch * chunk, (ch + 1) * chunk)
        v = v_ref[rs, :].astype(BF)
        mixed = jnp.zeros((chunk, c), F32)
        for h in range(N_HEADS):
            mixed = jnp.where(grp == h, _dot(w_heads[h], v), mixed)
        o_ref[rs, :] = u_ref[rs, :] * (mixed + bias)


def _sgu(u, v, layer, w, b, *, mix_len, chunk, tm):
    rows, c = u.shape
    assert rows % tm == 0 and tm % chunk == 0
    return pl.pallas_call(
        functools.partial(_sgu_kernel, mix_len=mix_len, chunk=chunk),
        out_shape=jax.ShapeDtypeStruct((rows, c), F32),
        grid=(rows // tm,),
        in_specs=[
            pl.BlockSpec((tm, c), lambda r: (r, 0)),
            pl.BlockSpec((tm, c), lambda r: (r, 0)),
            pl.BlockSpec((None, N_HEADS, chunk, chunk), lambda r: (layer, 0, 0, 0)),
            pl.BlockSpec((None, chunk, c), lambda r: (layer, 0, 0)),
        ],
        out_specs=pl.BlockSpec((tm, c), lambda r: (r, 0)),
        compiler_params=_params(("parallel",)),
        name="sgu",
    )(u, v, w, b)


def _mla_prompt_kernel(q_ref, k_ref, v_ref, wuv_ref, o_ref, m_ref, l_ref, acc_ref, *, tq):
    qi = pl.program_id(1)
    m_ref[...] = jnp.full(m_ref.shape, NEG, F32)
    l_ref[...] = jnp.zeros_like(l_ref)
    acc_ref[...] = jnp.zeros_like(acc_ref)

    def step(kj, diagonal):
        start = pl.multiple_of(kj * tq, tq)
        k = k_ref[pl.ds(start, tq), :]
        v = v_ref[pl.ds(start, tq), :]
        if diagonal:
            row = lax.broadcasted_iota(jnp.int32, (tq, tq), 0)
            col = lax.broadcasted_iota(jnp.int32, (tq, tq), 1)
            causal = col <= row
        for h in range(N_HEADS):
            hs = slice(h * MLA_HEAD_PAD, (h + 1) * MLA_HEAD_PAD)
            s = _dot_nt(q_ref[:, hs], k[:, hs]) * MLA_SCALE
            if diagonal:
                s = jnp.where(causal, s, NEG)
            m_old = m_ref[h]
            m_new = jnp.maximum(m_old, jnp.max(s, axis=-1, keepdims=True))
            alpha = jnp.exp(m_old - m_new)
            p = jnp.exp(s - m_new)
            l_ref[h] = alpha * l_ref[h] + jnp.sum(p, axis=-1, keepdims=True)
            acc_ref[h] = alpha * acc_ref[h] + _dot(p.astype(BF), v)
            m_ref[h] = m_new

    def body(kj, carry):
        step(kj, False)
        return carry

    lax.fori_loop(0, qi, body, 0)
    step(qi, True)
    out = jnp.zeros(o_ref.shape, F32)
    for h in range(N_HEADS):
        o_lat = acc_ref[h] / l_ref[h]
        out = out + _dot(o_lat.astype(BF), wuv_ref[h])
    o_ref[...] = out


def _mla_prompt(qmla, kmla, latb, layer, wuv, *, batch, seq):
    tq = min(256, seq)
    nq = seq // tq
    rows = batch * seq
    return pl.pallas_call(
        functools.partial(_mla_prompt_kernel, tq=tq),
        out_shape=jax.ShapeDtypeStruct((rows, GROUP_WIDTH), F32),
        grid=(batch, nq),
        in_specs=[
            pl.BlockSpec((tq, qmla.shape[1]), lambda b, q: (b * nq + q, 0)),
            pl.BlockSpec((seq, kmla.shape[1]), lambda b, q: (b, 0)),
            pl.BlockSpec((seq, latb.shape[1]), lambda b, q: (b, 0)),
            pl.BlockSpec((None,) + wuv.shape[1:], lambda b, q: (layer, 0, 0, 0)),
        ],
        out_specs=pl.BlockSpec((tq, GROUP_WIDTH), lambda b, q: (b * nq + q, 0)),
        scratch_shapes=[pltpu.VMEM((N_HEADS, tq, 1), F32), pltpu.VMEM((N_HEADS, tq, 1), F32),
                        pltpu.VMEM((N_HEADS, tq, latb.shape[1]), F32)],
        compiler_params=_params(("parallel", "arbitrary")),
        name="mla_prompt",
    )(qmla, kmla, latb, wuv)


def _topk_select(scores, n_cand_static, n_valid, topk):
    lane = lax.broadcasted_iota(jnp.int32, scores.shape, 1)
    cnt = jnp.zeros(scores.shape, jnp.int32)
    for j in range(n_cand_static):
        cj = scores[:, j:j + 1]
        beats = jnp.where((cj > scores) | ((cj == scores) & (lane > j)), 1, 0)
        if not isinstance(n_valid, int):
            beats = beats * jnp.where(n_valid > j, 1, 0)
        cnt = cnt + beats
    sel = (lane < n_valid) & (cnt < topk)
    return jnp.where(sel, 1.0, 0.0)


def _moba_prompt_kernel(q_ref, k_ref, v_ref, o_ref, sel_ref, m_ref, l_ref, acc_ref, *, nblk, topk):
    qi = pl.program_id(1)
    tq, c = q_ref.shape
    seq = k_ref.shape[0]
    q = q_ref[...]
    grp = _lane_group((tq, c), HEAD_DIM)
    q_heads = [jnp.where(grp == h, q, jnp.zeros_like(q)) for h in range(N_HEADS)]

    brow = lax.broadcasted_iota(jnp.int32, (LANES, seq), 0)
    tcol = lax.broadcasted_iota(jnp.int32, (LANES, seq), 1)
    avg = jnp.where(tcol // MOBA_BLOCK == brow, 1.0 / MOBA_BLOCK, 0.0).astype(BF)
    means = _dot(avg, k_ref[...]).astype(BF)

    row = lax.broadcasted_iota(jnp.int32, (tq, tq), 0)
    col = lax.broadcasted_iota(jnp.int32, (tq, tq), 1)
    causal = col <= row
    start = pl.multiple_of(qi * tq, tq)
    k_own = k_ref[pl.ds(start, tq), :]
    v_own = v_ref[pl.ds(start, tq), :]
    for h in range(N_HEADS):
        sel_ref[h] = _topk_select(_dot_nt(q_heads[h], means), nblk - 1, qi, topk)
        s = jnp.where(causal, _dot_nt(q_heads[h], k_own), NEG)
        m = jnp.max(s, axis=-1, keepdims=True)
        p = jnp.exp(s - m)
        m_ref[h] = m
        l_ref[h] = jnp.sum(p, axis=-1, keepdims=True)
        acc_ref[h] = _dot(p.astype(BF), v_own)

    for j in range(nblk - 1):
        @pl.when(j < qi)
        def _(j=j):
            k = k_ref[j * tq:(j + 1) * tq, :]
            v = v_ref[j * tq:(j + 1) * tq, :]
            for h in range(N_HEADS):
                chosen = sel_ref[h][:, j:j + 1] > 0.0
                s = jnp.where(chosen, _dot_nt(q_heads[h], k), NEG)
                m_old = m_ref[h]
                m_new = jnp.maximum(m_old, jnp.max(s, axis=-1, keepdims=True))
                alpha = jnp.exp(m_old - m_new)
                p = jnp.exp(s - m_new)
                l_ref[h] = alpha * l_ref[h] + jnp.sum(p, axis=-1, keepdims=True)
                acc_ref[h] = alpha * acc_ref[h] + _dot(p.astype(BF), v)
                m_ref[h] = m_new

    out = jnp.zeros((tq, c), F32)
    for h in range(N_HEADS):
        out = jnp.where(grp == h, acc_ref[h] / l_ref[h], out)
    o_ref[...] = out


def _moba_prompt(qd, kdb, vdb, *, batch, seq):
    tq = MOBA_BLOCK
    assert seq % tq == 0
    nblk = seq // tq
    n_cand = (seq - 1) // MOBA_BLOCK
    topk = min(MOBA_TOPK, n_cand)
    rows, c = qd.shape
    return pl.pallas_call(
        functools.partial(_moba_prompt_kernel, nblk=nblk, topk=topk),
        out_shape=jax.ShapeDtypeStruct((rows, c), F32),
        grid=(batch, nblk),
        in_specs=[
            pl.BlockSpec((tq, c), lambda b, q: (b * nblk + q, 0)),
            pl.BlockSpec((seq, c), lambda b, q: (b, 0)),
            pl.BlockSpec((seq, c), lambda b, q: (b, 0)),
        ],
        out_specs=pl.BlockSpec((tq, c), lambda b, q: (b * nblk + q, 0)),
        scratch_shapes=[pltpu.VMEM((N_HEADS, tq, LANES), F32), pltpu.VMEM((N_HEADS, tq, 1), F32),
                        pltpu.VMEM((N_HEADS, tq, 1), F32), pltpu.VMEM((N_HEADS, tq, c), F32)],
        compiler_params=_params(("parallel", "arbitrary")),
        name="moba_prompt",
    )(qd, kdb, vdb)


def _stack_heads(q8, width):
    grp = _lane_group(q8.shape, width)
    return jnp.concatenate([jnp.where(grp == h, q8, 0.0) for h in range(N_HEADS)], axis=0)


def _new_key_mask(rows, dec):
    r = lax.broadcasted_iota(jnp.int32, (rows, LANES), 0) % Q_PAD
    col = lax.broadcasted_iota(jnp.int32, (rows, LANES), 1)
    return (col <= r) & (col < dec)


def _pad_rows(x, rows):
    return jnp.concatenate([x, jnp.zeros((rows - x.shape[0], x.shape[1]), x.dtype)], axis=0)


def _mla_sample_kernel(pt_ref, q_ref, qr_ref, knew_ref, lnew_ref, wuk_ref, gk_ref, gm_ref, wuv_ref,
                       *rest, pps, nchunks, dec):
    lat_refs, krt_refs = rest[:pps], rest[pps:2 * pps]
    o_ref, qs_ref, m_ref, l_ref, acc_ref = rest[2 * pps:]
    c = pl.program_id(1)

    @pl.when(c == 0)
    def _():
        qs_ref[...] = _stack_heads(q_ref[...], MLA_HEAD_PAD).astype(BF)
        m_ref[...] = jnp.full(m_ref.shape, NEG, F32)
        l_ref[...] = jnp.zeros_like(l_ref)
        acc_ref[...] = jnp.zeros_like(acc_ref)

    qs = qs_ref[...]
    qr = qr_ref[...].astype(BF)

    def update(s, v_b):
        m_old = m_ref[...]
        m_new = jnp.maximum(m_old, jnp.max(s, axis=-1, keepdims=True))
        alpha = jnp.exp(m_old - m_new)
        p = jnp.exp(s - m_new)
        l_ref[...] = alpha * l_ref[...] + jnp.sum(p, axis=-1, keepdims=True)
        acc_ref[...] = alpha * acc_ref[...] + _dot(p.astype(BF), v_b)
        m_ref[...] = m_new

    for i in range(pps):
        lat_b = lat_refs[i][...].astype(BF)
        kn = _group_rms(_dot(lat_b, wuk_ref[...]), gm_ref, gk_ref[...]).astype(BF)
        s = _dot_nt(qs, kn) + _dot(qr, krt_refs[i][...].astype(BF))
        update(s * MLA_SCALE, lat_b)

    @pl.when(c == nchunks - 1)
    def _():
        k_new = _pad_rows(knew_ref[...], LANES).astype(BF)
        l_new = _pad_rows(lnew_ref[...], LANES).astype(BF)
        s = _dot_nt(qs, k_new) * MLA_SCALE
        update(jnp.where(_new_key_mask(qs.shape[0], dec), s, NEG), l_new)
        o_lat = acc_ref[...] / l_ref[...]
        out = jnp.zeros(o_ref.shape, F32)
        for h in range(N_HEADS):
            out = out + _dot(o_lat[h * Q_PAD:(h + 1) * Q_PAD].astype(BF), wuv_ref[h])
        o_ref[...] = out


def _pages_per_step(n_pages):
    for pps in (8, 4, 2):
        if n_pages % pps == 0:
            return pps
    raise AssertionError("page count must be even")


def _mla_sample(page_table, q8, qr8, knew8, lnew8, cache_lat, cache_krt, layer, wp, *, dec):
    db, n_pages = page_table.shape
    pps = _pages_per_step(n_pages)
    nchunks = n_pages // pps
    page, rank = cache_lat.shape[2], cache_lat.shape[3]
    rope_w = cache_krt.shape[2]
    rows = N_HEADS * Q_PAD

    def bspec(a):
        return pl.BlockSpec((None,) + a.shape[1:], lambda b, c, pt: (b, 0, 0))

    def wspec(a):
        shape = a.shape[1:]
        return pl.BlockSpec((None,) + shape, lambda b, c, pt: (layer,) + (0,) * len(shape))

    def cspec(a):
        return pl.BlockSpec(a.shape, lambda b, c, pt: (0,) * a.ndim)

    def page_spec(shape, i):
        return pl.BlockSpec((None, None) + shape, lambda b, c, pt: (layer, pt[b, c * pps + i], 0, 0))

    in_specs = ([bspec(q8), bspec(qr8), bspec(knew8), bspec(lnew8),
                 wspec(wp['w_uk']), wspec(wp['gk']), cspec(wp['gm512']), wspec(wp['w_uv'])]
                + [page_spec((page, rank), i) for i in range(pps)]
                + [page_spec((rope_w, page), i) for i in range(pps)])
    return pl.pallas_call(
        functools.partial(_mla_sample_kernel, pps=pps, nchunks=nchunks, dec=dec),
        out_shape=jax.ShapeDtypeStruct((db, Q_PAD, GROUP_WIDTH), F32),
        grid_spec=pltpu.PrefetchScalarGridSpec(
            num_scalar_prefetch=1,
            grid=(db, nchunks),
            in_specs=in_specs,
            out_specs=pl.BlockSpec((None, Q_PAD, GROUP_WIDTH), lambda b, c, pt: (b, 0, 0)),
            scratch_shapes=[pltpu.VMEM((rows, q8.shape[2]), BF),
                            pltpu.VMEM((rows, 1), F32), pltpu.VMEM((rows, 1), F32),
                            pltpu.VMEM((rows, rank), F32)]),
        compiler_params=_params(("parallel", "arbitrary")),
        name="mla_sample",
    )(page_table, q8, qr8, knew8, lnew8, wp['w_uk'], wp['gk'], wp['gm512'], wp['w_uv'],
      *([cache_lat] * pps), *([cache_krt] * pps))


def _moba_sample_kernel(pt_ref, q_ref, knew_ref, vnew_ref, *rest, pps, nchunks, nblk, dec, topk):
    kt_refs, vt_refs = rest[:pps], rest[pps:2 * pps]
    o_ref, qs_ref, ss_ref, m_ref, l_ref, acc_ref = rest[2 * pps:]
    c = pl.program_id(1)
    ppb = 2
    bpc = pps // ppb
    page = kt_refs[0].shape[1]

    @pl.when(c == 0)
    def _():
        qs_ref[...] = _stack_heads(q_ref[...], HEAD_DIM).astype(BF)

    qs = qs_ref[...]
    for bi in range(bpc):
        j = c * bpc + bi
        pages = range(ppb * bi, ppb * (bi + 1))
        s = jnp.concatenate([_dot(qs, kt_refs[g][...].astype(BF)) for g in pages], axis=-1)
        ss_ref[j] = jnp.sum(s, axis=-1, keepdims=True) * (1.0 / MOBA_BLOCK)
        m = jnp.max(s, axis=-1, keepdims=True)
        p = jnp.exp(s - m)
        m_ref[j] = m
        l_ref[j] = jnp.sum(p, axis=-1, keepdims=True)
        acc = jnp.zeros(acc_ref.shape[1:], F32)
        for n, g in enumerate(pages):
            acc = acc + _dot_nt(p[:, n * page:(n + 1) * page].astype(BF), vt_refs[g][...].astype(BF))
        acc_ref[j] = acc

    @pl.when(c == nchunks - 1)
    def _():
        rows = qs.shape[0]
        lane = lax.broadcasted_iota(jnp.int32, (rows, LANES), 1)
        scores = jnp.zeros((rows, LANES), F32)
        for j in range(nblk):
            scores = jnp.where(lane == j, ss_ref[j], scores)
        sel = _topk_select(scores, nblk, nblk, topk)

        k_new = _pad_rows(knew_ref[...], LANES).astype(BF)
        v_new = _pad_rows(vnew_ref[...], LANES).astype(BF)
        s_own = jnp.where(_new_key_mask(rows, dec), _dot_nt(qs, k_new), NEG)
        m_all = jnp.max(s_own, axis=-1, keepdims=True)
        for j in range(nblk):
            m_all = jnp.maximum(m_all, jnp.where(sel[:, j:j + 1] > 0.0, m_ref[j], NEG))
        p_own = jnp.exp(s_own - m_all)
        l_all = jnp.sum(p_own, axis=-1, keepdims=True)
        o_all = _dot(p_own.astype(BF), v_new)
        for j in range(nblk):
            w = jnp.where(sel[:, j:j + 1] > 0.0, jnp.exp(m_ref[j] - m_all), 0.0)
            l_all = l_all + w * l_ref[j]
            o_all = o_all + w * acc_ref[j]
        o_all = o_all / l_all
        grp = _lane_group(o_ref.shape, HEAD_DIM)
        out = jnp.zeros(o_ref.shape, F32)
        for h in range(N_HEADS):
            out = jnp.where(grp == h, o_all[h * Q_PAD:(h + 1) * Q_PAD], out)
        o_ref[...] = out


def _moba_sample(page_table, q8, knew8, vnew8, cache_kt, cache_vt, layer, *, dec):
    db, n_pages = page_table.shape
    pps = _pages_per_step(n_pages)
    nchunks = n_pages // pps
    c, page = cache_kt.shape[2], cache_kt.shape[3]
    assert 2 * page == MOBA_BLOCK and dec <= Q_PAD
    nblk = n_pages // 2
    assert nblk <= LANES
    topk = min(MOBA_TOPK, nblk)
    rows = N_HEADS * Q_PAD

    def bspec():
        return pl.BlockSpec((None, Q_PAD, c), lambda b, ch, pt: (b, 0, 0))

    def page_spec(i):
        return pl.BlockSpec((None, None, c, page), lambda b, ch, pt: (layer, pt[b, ch * pps + i], 0, 0))

    return pl.pallas_call(
        functools.partial(_moba_sample_kernel, pps=pps, nchunks=nchunks, nblk=nblk, dec=dec, topk=topk),
        out_shape=jax.ShapeDtypeStruct((db, Q_PAD, c), F32),
        grid_spec=pltpu.PrefetchScalarGridSpec(
            num_scalar_prefetch=1,
            grid=(db, nchunks),
            in_specs=[bspec(), bspec(), bspec()] + [page_spec(i) for i in range(pps)] * 2,
            out_specs=pl.BlockSpec((None, Q_PAD, c), lambda b, ch, pt: (b, 0, 0)),
            scratch_shapes=[pltpu.VMEM((rows, c), BF), pltpu.VMEM((nblk, rows, 1), F32),
                            pltpu.VMEM((nblk, rows, 1), F32), pltpu.VMEM((nblk, rows, 1), F32),
                            pltpu.VMEM((nblk, rows, c), F32)]),
        compiler_params=_params(("parallel", "arbitrary")),
        name="moba_sample",
    )(page_table, q8, knew8, vnew8, *([cache_kt] * pps), *([cache_vt] * pps))


def _outproj_kernel(x_ref, a_ref, b_ref, c_ref, d_ref, w_ref, o_ref):
    mixed = jnp.concatenate([a_ref[...], b_ref[...], c_ref[...], d_ref[...]], axis=-1).astype(BF)
    o_ref[...] = x_ref[...] + _dot(mixed, w_ref[...])


def _outproj(x, a, b, c, d, layer, w_out):
    rows, dm = x.shape
    tm = _row_tile(rows, 512)
    gw = a.shape[1]
    return pl.pallas_call(
        _outproj_kernel,
        out_shape=jax.ShapeDtypeStruct((rows, dm), F32),
        grid=(rows // tm,),
        in_specs=[pl.BlockSpec((tm, dm), lambda r: (r, 0))]
                 + [pl.BlockSpec((tm, gw), lambda r: (r, 0))] * 4
                 + [pl.BlockSpec((None,) + w_out.shape[1:], lambda r: (layer, 0, 0))],
        out_specs=pl.BlockSpec((tm, dm), lambda r: (r, 0)),
        compiler_params=_params(("parallel",)),
        name="outproj",
    )(x, a, b, c, d, w_out)


def _ple_kernel(x_ref, pe_ref, g_ref, wg_ref, wp_ref, o_ref):
    x = x_ref[...]
    gate = jax.nn.sigmoid(_dot(_rms(x, g_ref[...]).astype(BF), wg_ref[...]))
    o_ref[...] = x + gate * _dot(pe_ref[...].astype(BF), wp_ref[...])


def _ple(x, pe, layer, g, w_gate, w_ple):
    rows, dm = x.shape
    tm = _row_tile(rows, 512)
    pd = pe.shape[-1]
    return pl.pallas_call(
        _ple_kernel,
        out_shape=jax.ShapeDtypeStruct((rows, dm), F32),
        grid=(rows // tm,),
        in_specs=[
            pl.BlockSpec((tm, dm), lambda r: (r, 0)),
            pl.BlockSpec((None, tm, pd), lambda r: (layer, r, 0)),
            pl.BlockSpec((None, 1, dm), lambda r: (layer, 0, 0)),
            pl.BlockSpec((None, dm, dm), lambda r: (layer, 0, 0)),
            pl.BlockSpec((None, pd, dm), lambda r: (layer, 0, 0)),
        ],
        out_specs=pl.BlockSpec((tm, dm), lambda r: (r, 0)),
        compiler_params=_params(("parallel",)),
        name="ple",
    )(x, pe, g, w_gate, w_ple)


def _rope_tables(pos):
    pos = pos.astype(F32)[:, None]
    n = pos.shape[0]

    def cs(half):
        inv_freq = ROPE_THETA ** (-jnp.arange(half, dtype=F32) / half)
        ang = pos * inv_freq[None, :]
        return jnp.cos(ang), jnp.sin(ang)

    c16, s16 = cs(MLA_ROPE // 2)
    one = lambda w: jnp.ones((n, w), F32)
    zero = lambda w: jnp.zeros((n, w), F32)
    cos_m = jnp.concatenate([one(64), c16, c16, one(32)], -1)
    sa_m = jnp.concatenate([zero(64), -s16, zero(48)], -1)
    sb_m = jnp.concatenate([zero(80), s16, zero(32)], -1)
    c32, s32 = cs(HEAD_DIM // 2)
    cos_d = jnp.concatenate([c32] * 4, -1)
    sa_d = jnp.concatenate([-s32, zero(32), -s32, zero(32)], -1)
    sb_d = jnp.concatenate([zero(32), s32, zero(32), s32], -1)
    return jnp.concatenate([cos_m, sa_m, sb_m, cos_d, sa_d, sb_d], -1)


def _group_mean_matrix_512():
    g = np.zeros((4 * MLA_HEAD_PAD, 4 * MLA_HEAD_PAD), np.float32)
    for h in range(N_HEADS):
        o = h * MLA_HEAD_PAD
        g[o:o + MLA_NOPE, o:o + MLA_NOPE] = 1.0 / MLA_NOPE
        g[o + MLA_NOPE:o + MLA_NOPE + MLA_ROPE, o + MLA_NOPE:o + MLA_NOPE + MLA_ROPE] = 1.0 / MLA_ROPE
    return jnp.asarray(g, BF)


def _group_mean_matrix_256():
    g = np.zeros((GROUP_WIDTH, GROUP_WIDTH), np.float32)
    for h in range(N_HEADS):
        o = h * HEAD_DIM
        g[o:o + HEAD_DIM, o:o + HEAD_DIM] = 1.0 / HEAD_DIM
    return jnp.asarray(g, BF)


def _prepare_weights(w_in, mix_norm, mla_q_norm, mla_w_uq, mla_kv_norm, mla_w_uk, mla_w_uv,
                     mla_q_nope_norm, mla_q_rope_norm, mla_k_nope_norm, mla_k_rope_norm,
                     moba_q_norm, moba_k_norm):
    nl, d, _ = w_in.shape
    zc = lambda w: jnp.zeros((nl, d, w), w_in.dtype)
    w_in_p = jnp.concatenate([w_in[..., :384], zc(_KR_LANE), w_in[..., 384:416],
                              zc(128 - _KR_LANE - MLA_ROPE), w_in[..., 416:]], axis=-1).astype(BF)
    assert w_in_p.shape[-1] == _C_END
    pad_h = lambda w: jnp.pad(w, ((0, 0), (0, 0), (0, 0), (0, MLA_HEAD_PAD - w.shape[-1])))
    w_uq = pad_h(mla_w_uq).reshape(nl, mla_w_uq.shape[1], -1).astype(BF)
    w_uk = pad_h(mla_w_uk).reshape(nl, mla_w_uk.shape[1], -1).astype(BF)
    eye = jnp.eye(N_HEADS, dtype=mla_w_uv.dtype)
    w_uv = jnp.einsum('lrhd,hg->lhrgd', mla_w_uv, eye).reshape(nl, N_HEADS, mla_w_uv.shape[1], -1).astype(BF)
    z = lambda w: jnp.zeros((nl, w), F32)
    gq = jnp.tile(jnp.concatenate([mla_q_nope_norm, mla_q_rope_norm, z(32)], -1), (1, N_HEADS))
    gk = jnp.tile(jnp.concatenate([mla_k_nope_norm, z(64)], -1), (1, N_HEADS))
    gkr = jnp.concatenate([z(_KR_LANE), mla_k_rope_norm, z(128 - _KR_LANE - MLA_ROPE)], -1)
    row = lambda a: a[:, None, :].astype(F32)
    return {
        'w_in': w_in_p, 'mix_norm': row(mix_norm), 'mla_q_norm': row(mla_q_norm), 'w_uq': w_uq,
        'gq': row(gq), 'mla_kv_norm': row(mla_kv_norm), 'w_uk': w_uk, 'gk': row(gk), 'gkr': row(gkr),
        'w_uv': w_uv, 'gm512': _group_mean_matrix_512(), 'gm256': _group_mean_matrix_256(),
        'dqn': row(jnp.tile(moba_q_norm, (1, N_HEADS))), 'dkn': row(jnp.tile(moba_k_norm, (1, N_HEADS))),
    }


def _head_bias(b):
    return jnp.repeat(jnp.swapaxes(b, 1, 2), HEAD_DIM, axis=-1)


def _pad_queries(a, db, dec):
    a = a.astype(F32).reshape(db, dec, a.shape[-1])
    return jnp.pad(a, ((0, 0), (0, Q_PAD - dec), (0, 0)))


def kernel(x_prompt, x_sample, cache_mla_latent, cache_mla_krope, cache_moba_k, cache_moba_v, state_pool, page_table, p_prompt, p_sample, ffn1_norm, ffn1_w_gate, ffn1_w_up, ffn1_w_down, mix_norm, w_in, mla_q_norm, mla_w_uq, mla_kv_norm, mla_w_uk, mla_w_uv, mla_q_nope_norm, mla_q_rope_norm, mla_k_nope_norm, mla_k_rope_norm, pool_w, pool_scale, sgu_w, sgu_b, moba_q_norm, moba_k_norm, w_out, ffn2_norm, ffn2_w_gate, ffn2_w_up, ffn2_w_down, ple_w, ple_gate_norm, ple_w_gate):
    batch, seq, d = x_prompt.shape
    db, dec, _ = x_sample.shape
    depth = w_in.shape[0]
    n_pages = page_table.shape[1]
    page = cache_mla_latent.shape[2]
    past_len = n_pages * page
    assert past_len % MOBA_BLOCK == 0 and seq % MOBA_BLOCK == 0 and dec <= Q_PAD
    rows_p, rows_s = batch * seq, db * dec

    wp = _prepare_weights(w_in, mix_norm, mla_q_norm, mla_w_uq, mla_kv_norm, mla_w_uk, mla_w_uv,
                          mla_q_nope_norm, mla_q_rope_norm, mla_k_nope_norm, mla_k_rope_norm,
                          moba_q_norm, moba_k_norm)
    row = lambda a: a[:, None, :].astype(F32)
    bf = lambda a: a.astype(BF)
    f1 = (row(ffn1_norm), bf(ffn1_w_gate), bf(ffn1_w_up), bf(ffn1_w_down))
    f2 = (row(ffn2_norm), bf(ffn2_w_gate), bf(ffn2_w_up), bf(ffn2_w_down))
    w_out_b, ple_w_b, ple_wg_b, ple_gn = bf(w_out), bf(ple_w), bf(ple_w_gate), row(ple_gate_norm)
    eye = jnp.eye(len(POOL_WINDOWS), dtype=pool_w.dtype)
    pool_w_bd = bf(jnp.einsum('lgcd,gh->lgchd', pool_w, eye).reshape(depth, GROUP_WIDTH, GROUP_WIDTH))
    pool_sc = row(pool_scale)

    mix_p = SGU_CHUNK if seq % SGU_CHUNK == 0 else seq
    sgu_w_p = sgu_w[:, :, :mix_p, :mix_p]
    sgu_b_p = _head_bias(sgu_b[:, :, :mix_p])
    mix_s = SGU_CHUNK if dec % SGU_CHUNK == 0 else dec
    sgu_w_s = jnp.tile(sgu_w[:, :, :mix_s, :mix_s], (1, 1, rows_s // mix_s, rows_s // mix_s))
    sgu_b_s = jnp.tile(_head_bias(sgu_b[:, :, :mix_s]), (1, rows_s // mix_s, 1))

    tab_p = _rope_tables(jnp.arange(seq))
    tab_s = _rope_tables(past_len + (jnp.arange(rows_s) % dec))

    page_major = lambda a: jnp.transpose(a, (0, 1, 3, 4, 2)).reshape(a.shape[:2] + (-1, a.shape[2]))
    cache_kt, cache_vt = page_major(cache_moba_k), page_major(cache_moba_v)
    cache_krt = jnp.swapaxes(cache_mla_krope, 2, 3)
    hist_pad = -(-(POOL_HIST + dec) // 8) * 8

    xp = x_prompt.reshape(rows_p, d)
    xs = x_sample.reshape(rows_s, d)
    pe_p = p_prompt.reshape(depth, rows_p, -1)
    pe_s = p_sample.reshape(depth, rows_s, -1)
    st_p = [[] for _ in range(5)]
    st_s = [[] for _ in range(6)]
    for i in range(depth):
        xp = _ffn(xp, i, *f1)
        (qmla, kmla, lat, latb, kr, zb, u, v, qd, kd, kdb, vd, vdb) = _inproj(xp, i, wp, tab_p)
        out_a = _mla_prompt(qmla, kmla, latb, i, wp['w_uv'], batch=batch, seq=seq)
        out_b = _pool(zb, i, pool_w_bd, pool_sc, seq=seq, n_seq_blk=1, pos_start=0)
        out_c = _sgu(u, v, i, sgu_w_p, sgu_b_p, mix_len=mix_p, chunk=mix_p, tm=_row_tile(seq, 1024))
        out_d = _moba_prompt(qd, kdb, vdb, batch=batch, seq=seq)
        xp = _outproj(xp, out_a, out_b, out_c, out_d, i, w_out_b)
        xp = _ffn(xp, i, *f2)
        xp = _ple(xp, pe_p, i, ple_gn, ple_wg_b, ple_w_b)
        st_p[0].append(lat.reshape(batch, seq, -1))
        st_p[1].append(kr.reshape(batch, seq, -1))
        st_p[2].append(kd.reshape(batch, seq, N_HEADS, HEAD_DIM))
        st_p[3].append(vd.reshape(batch, seq, N_HEADS, HEAD_DIM))
        st_p[4].append(zb.reshape(batch, seq, -1)[:, seq - POOL_HIST:])

        xs = _ffn(xs, i, *f1)
        (qmla, kmla, lat, latb, kr, zb, u, v, qd, kd, kdb, vd, vdb) = _inproj(xs, i, wp, tab_s)
        q8 = _pad_queries(qmla, db, dec)
        qr8 = q8.reshape(db, Q_PAD, N_HEADS, MLA_HEAD_PAD)[..., MLA_NOPE:MLA_NOPE + MLA_ROPE]
        qr8 = jnp.swapaxes(qr8, 1, 2).reshape(db, N_HEADS * Q_PAD, MLA_ROPE)
        out_a = _mla_sample(page_table, q8, qr8, _pad_queries(kmla, db, dec), _pad_queries(latb, db, dec),
                            cache_mla_latent, cache_krt, i, wp, dec=dec)
        out_a = out_a[:, :dec].reshape(rows_s, -1)
        full = jnp.concatenate([state_pool[i], zb.reshape(db, dec, -1)], axis=1)
        full_p = jnp.pad(full, ((0, 0), (0, hist_pad - full.shape[1]), (0, 0)))
        out_b = _pool(full_p.reshape(db * hist_pad, -1), i, pool_w_bd, pool_sc, seq=hist_pad, n_seq_blk=db,
                      pos_start=past_len - POOL_HIST)
        out_b = out_b.reshape(db, hist_pad, -1)[:, POOL_HIST:POOL_HIST + dec].reshape(rows_s, -1)
        out_c = _sgu(u, v, i, sgu_w_s, sgu_b_s, mix_len=mix_s, chunk=rows_s, tm=rows_s)
        out_d = _moba_sample(page_table, _pad_queries(qd, db, dec), _pad_queries(kdb, db, dec),
                             _pad_queries(vdb, db, dec), cache_kt, cache_vt, i, dec=dec)
        out_d = out_d[:, :dec].reshape(rows_s, -1)
        xs = _outproj(xs, out_a, out_b, out_c, out_d, i, w_out_b)
        xs = _ffn(xs, i, *f2)
        xs = _ple(xs, pe_s, i, ple_gn, ple_wg_b, ple_w_b)
        st_s[0].append(lat.reshape(db, dec, -1))
        st_s[1].append(kr.reshape(db, dec, -1))
        st_s[2].append(kd.reshape(db, dec, N_HEADS, HEAD_DIM))
        st_s[3].append(vd.reshape(db, dec, N_HEADS, HEAD_DIM))
        st_s[4].append(full[:, full.shape[1] - POOL_HIST:])
        st_s[5].append(v.reshape(db, dec, -1))

    sp = [jnp.stack(s) for s in st_p]
    ss = [jnp.stack(s) for s in st_s]
    return (xp.reshape(batch, seq, d), xs.reshape(db, dec, d), sp[0], sp[1], ss[0], ss[1],
            sp[2], sp[3], ss[2], ss[3], sp[4], ss[4], ss[5])
```

```python
import functools
import math

import numpy as np
import jax
import jax.numpy as jnp
from jax import lax
from jax.experimental import pallas as pl
from jax.experimental.pallas import tpu as pltpu

F32 = jnp.float32
BF = jnp.bfloat16

EPS = 1e-6
ROPE_THETA = 10000.0
HEAD_DIM = 64
N_HEADS = 4
GROUP_WIDTH = N_HEADS * HEAD_DIM
MLA_NOPE = HEAD_DIM
MLA_ROPE = HEAD_DIM // 2
MLA_HEAD_PAD = 128
MLA_SCALE = 1.0 / math.sqrt(MLA_NOPE + MLA_ROPE)
POOL_WINDOWS = (2, 4, 8, 16)
POOL_HIST = 15
SGU_CHUNK = 128
MOBA_BLOCK = 256
MOBA_TOPK = 3
MOBA_SCALE = 1.0 / math.sqrt(HEAD_DIM)
NEG = -1e30
LANES = 128
Q_PAD = 8
VMEM_LIMIT = 56 * 1024 * 1024


def _dot(a, b):
    return jnp.dot(a, b, preferred_element_type=F32)


def _dot_nt(a, b):
    return lax.dot_general(a, b, (((1,), (1,)), ((), ())), preferred_element_type=F32)


def _rms(x, g):
    ms = jnp.mean(x * x, axis=-1, keepdims=True)
    return x * lax.rsqrt(ms + EPS) * g


def _group_rms(x, gmat_ref, gain):
    ms = _dot((x * x).astype(BF), gmat_ref[...])
    return x * lax.rsqrt(ms + EPS) * gain


def _rope(x, cos, sin_a, sin_b, half):
    w = x.shape[-1]
    return x * cos + pltpu.roll(x, w - half, 1) * sin_a + pltpu.roll(x, half, 1) * sin_b


def _lane_group(shape, width):
    lane = lax.broadcasted_iota(jnp.int32, shape, len(shape) - 1)
    return lane // width


def _tile_lanes(t, n):
    return t if n == 1 else jnp.concatenate([t] * n, axis=-1)


def _params(sem, vmem=VMEM_LIMIT):
    return pltpu.CompilerParams(dimension_semantics=sem, vmem_limit_bytes=vmem)


def _row_tile(rows, want):
    tm = min(rows, want)
    assert rows % tm == 0, (rows, tm)
    return tm


def _ffn_kernel(x_ref, g_ref, wg_ref, wu_ref, wd_ref, o_ref, hn_ref, acc_ref, *, nf):
    f = pl.program_id(1)

    @pl.when(f == 0)
    def _():
        hn_ref[...] = _rms(x_ref[...], g_ref[...]).astype(BF)
        acc_ref[...] = jnp.zeros_like(acc_ref)

    hn = hn_ref[...]
    gate = _dot(hn, wg_ref[...])
    up = _dot(hn, wu_ref[...])
    h = (gate * jax.nn.sigmoid(gate)) * up
    acc_ref[...] += _dot(h.astype(BF), wd_ref[...])

    @pl.when(f == nf - 1)
    def _():
        o_ref[...] = x_ref[...] + 0.5 * acc_ref[...]


def _ffn(x, layer, g, wg, wu, wd):
    rows, d = x.shape
    dff = wg.shape[-1]
    tm = _row_tile(rows, 512)
    tf = dff // 2 if (dff // 2) % LANES == 0 else dff
    nf = dff // tf
    return pl.pallas_call(
        functools.partial(_ffn_kernel, nf=nf),
        out_shape=jax.ShapeDtypeStruct((rows, d), F32),
        grid=(rows // tm, nf),
        in_specs=[
            pl.BlockSpec((tm, d), lambda r, f: (r, 0)),
            pl.BlockSpec((None, 1, d), lambda r, f: (layer, 0, 0)),
            pl.BlockSpec((None, d, tf), lambda r, f: (layer, 0, f)),
            pl.BlockSpec((None, d, tf), lambda r, f: (layer, 0, f)),
            pl.BlockSpec((None, tf, d), lambda r, f: (layer, f, 0)),
        ],
        out_specs=pl.BlockSpec((tm, d), lambda r, f: (r, 0)),
        scratch_shapes=[pltpu.VMEM((tm, d), BF), pltpu.VMEM((tm, d), F32)],
        compiler_params=_params(("parallel", "arbitrary")),
        name="ffn",
    )(x, g, wg, wu, wd)


_C_CQ, _C_CKV, _C_KR, _C_ZB, _C_ZU, _C_ZV, _C_ZQ, _C_ZK, _C_ZVD, _C_END = (
    0, 256, 384, 512, 768, 1024, 1280, 1536, 1792, 2048)
_KR_LANE = 64


def _inproj_kernel(x_ref, g_ref, w_ref, qn_ref, wuq_ref, gq_ref, kvn_ref, wuk_ref, gk_ref,
                   gkr_ref, gm512_ref, gm256_ref, dqn_ref, dkn_ref, rt_ref,
                   qmla_ref, kmla_ref, lat_ref, latb_ref, kr_ref, zb_ref, u_ref, v_ref,
                   qd_ref, kd_ref, kdb_ref, vd_ref, vdb_ref):
    hn = _rms(x_ref[...], g_ref[...]).astype(BF)
    z = _dot(hn, w_ref[...])

    rt = rt_ref[...]
    cos_m, sa_m, sb_m = rt[:, 0:128], rt[:, 128:256], rt[:, 256:384]
    cos_d, sa_d, sb_d = rt[:, 384:512], rt[:, 512:640], rt[:, 640:768]

    cq = _rms(z[:, _C_CQ:_C_CKV], qn_ref[...])
    qa = _dot(cq.astype(BF), wuq_ref[...])
    qa = _group_rms(qa, gm512_ref, gq_ref[...])
    qa = _rope(qa, _tile_lanes(cos_m, N_HEADS), _tile_lanes(sa_m, N_HEADS),
               _tile_lanes(sb_m, N_HEADS), MLA_ROPE // 2)
    qmla_ref[...] = qa.astype(BF)

    lat = _rms(z[:, _C_CKV:_C_KR], kvn_ref[...])
    lat_ref[...] = lat
    lat_b = lat.astype(BF)
    latb_ref[...] = lat_b

    kr_raw = z[:, _C_KR:_C_ZB]
    ms = jnp.sum(kr_raw * kr_raw, axis=-1, keepdims=True) * (1.0 / MLA_ROPE)
    kr = kr_raw * lax.rsqrt(ms + EPS) * gkr_ref[...]
    kr = _rope(kr, cos_m, sa_m, sb_m, MLA_ROPE // 2)
    kr_ref[...] = kr[:, _KR_LANE:_KR_LANE + MLA_ROPE]

    kn = _group_rms(_dot(lat_b, wuk_ref[...]), gm512_ref, gk_ref[...])
    kmla_ref[...] = (kn + _tile_lanes(kr, N_HEADS)).astype(BF)

    zb_ref[...] = z[:, _C_ZB:_C_ZU]
    u_ref[...] = jax.nn.gelu(z[:, _C_ZU:_C_ZV])
    v_ref[...] = jax.nn.gelu(z[:, _C_ZV:_C_ZQ])

    qd = _group_rms(z[:, _C_ZQ:_C_ZK], gm256_ref, dqn_ref[...])
    qd = _rope(qd, _tile_lanes(cos_d, 2), _tile_lanes(sa_d, 2), _tile_lanes(sb_d, 2), HEAD_DIM // 2)
    qd_ref[...] = (qd * MOBA_SCALE).astype(BF)
    kd = _group_rms(z[:, _C_ZK:_C_ZVD], gm256_ref, dkn_ref[...])
    kd = _rope(kd, _tile_lanes(cos_d, 2), _tile_lanes(sa_d, 2), _tile_lanes(sb_d, 2), HEAD_DIM // 2)
    kd_ref[...] = kd
    kdb_ref[...] = kd.astype(BF)
    vd = z[:, _C_ZVD:_C_END]
    vd_ref[...] = vd
    vdb_ref[...] = vd.astype(BF)


def _inproj(x, layer, wp, rope_tab):
    rows, d = x.shape
    tm = _row_tile(rows, 512)
    n_tab = rope_tab.shape[0] // tm
    assert rope_tab.shape[0] % tm == 0

    def wspec(a):
        shape = a.shape[1:]
        return pl.BlockSpec((None,) + shape, lambda r: (layer,) + (0,) * len(shape))

    def cspec(a):
        return pl.BlockSpec(a.shape, lambda r: (0,) * a.ndim)

    def ospec(w):
        return pl.BlockSpec((tm, w), lambda r: (r, 0))

    outs = [(4 * MLA_HEAD_PAD, BF), (4 * MLA_HEAD_PAD, BF), (128, F32), (128, BF), (MLA_ROPE, F32),
            (256, F32), (256, F32), (256, F32), (256, BF), (256, F32), (256, BF), (256, F32), (256, BF)]
    return pl.pallas_call(
        _inproj_kernel,
        out_shape=[jax.ShapeDtypeStruct((rows, w), dt) for w, dt in outs],
        grid=(rows // tm,),
        in_specs=[
            pl.BlockSpec((tm, d), lambda r: (r, 0)),
            wspec(wp['mix_norm']), wspec(wp['w_in']), wspec(wp['mla_q_norm']), wspec(wp['w_uq']),
            wspec(wp['gq']), wspec(wp['mla_kv_norm']), wspec(wp['w_uk']), wspec(wp['gk']),
            wspec(wp['gkr']), cspec(wp['gm512']), cspec(wp['gm256']), wspec(wp['dqn']), wspec(wp['dkn']),
            pl.BlockSpec((tm, rope_tab.shape[1]), lambda r: (r % n_tab, 0)),
        ],
        out_specs=[ospec(w) for w, _ in outs],
        compiler_params=_params(("parallel",)),
        name="inproj",
    )(x, wp['mix_norm'], wp['w_in'], wp['mla_q_norm'], wp['w_uq'], wp['gq'], wp['mla_kv_norm'],
      wp['w_uk'], wp['gk'], wp['gkr'], wp['gm512'], wp['gm256'], wp['dqn'], wp['dkn'], rope_tab)


def _pool_kernel(x_ref, w_ref, sc_ref, o_ref, *, seq, pos_start):
    x = x_ref[...]
    rows, c = x.shape
    t = lax.broadcasted_iota(jnp.int32, (rows // seq, seq, c), 1).reshape(rows, c)

    def shifted(a, k):
        return jnp.where(t >= k, pltpu.roll(a, k, 0), 0.0)

    s2 = x + shifted(x, 1)
    s4 = s2 + shifted(s2, 2)
    s8 = s4 + shifted(s4, 4)
    s16 = s8 + shifted(s8, 8)
    grp = _lane_group((rows, c), c // len(POOL_WINDOWS))
    win = jnp.where(grp == 0, s2, jnp.where(grp == 1, s4, jnp.where(grp == 2, s8, s16)))
    wsize = jnp.where(grp == 0, POOL_WINDOWS[0], jnp.where(grp == 1, POOL_WINDOWS[1],
                      jnp.where(grp == 2, POOL_WINDOWS[2], POOL_WINDOWS[3])))
    count = jnp.maximum(jnp.minimum(t + (pos_start + 1), wsize), 1).astype(F32)
    pooled = win / count - x
    o_ref[...] = _dot(pooled.astype(BF), w_ref[...]) * sc_ref[...]


def _pool(zb, layer, w_bd, scale, *, seq, n_seq_blk, pos_start):
    rows, c = zb.shape
    tm = seq * n_seq_blk
    assert rows % tm == 0
    return pl.pallas_call(
        functools.partial(_pool_kernel, seq=seq, pos_start=pos_start),
        out_shape=jax.ShapeDtypeStruct((rows, c), F32),
        grid=(rows // tm,),
        in_specs=[
            pl.BlockSpec((tm, c), lambda r: (r, 0)),
            pl.BlockSpec((None, c, c), lambda r: (layer, 0, 0)),
            pl.BlockSpec((None, 1, c), lambda r: (layer, 0, 0)),
        ],
        out_specs=pl.BlockSpec((tm, c), lambda r: (r, 0)),
        compiler_params=_params(("parallel",)),
        name="pool",
    )(zb, w_bd, scale)


def _sgu_kernel(u_ref, v_ref, w_ref, b_ref, o_ref, *, mix_len, chunk):
    tm, c = u_ref.shape
    r = lax.broadcasted_iota(jnp.int32, (chunk, chunk), 0)
    col = lax.broadcasted_iota(jnp.int32, (chunk, chunk), 1)
    keep = (r // mix_len == col // mix_len) & (col <= r)
    grp = _lane_group((chunk, c), HEAD_DIM)
    w_heads = [jnp.where(keep, w_ref[h], 0.0).astype(BF) for h in range(N_HEADS)]
    bias = b_ref[...]
    for ch in range(tm // chunk):
        rs = slice(ch * chunk, (ch + 1) * chunk)
        v = v_ref[rs, :].astype(BF)
        mixed = jnp.zeros((chunk, c), F32)
        for h in range(N_HEADS):
            mixed = jnp.where(grp == h, _dot(w_heads[h], v), mixed)
        o_ref[rs, :] = u_ref[rs, :] * (mixed + bias)


def _sgu(u, v, layer, w, b, *, mix_len, chunk, tm):
    rows, c = u.shape
    assert rows % tm == 0 and tm % chunk == 0
    return pl.pallas_call(
        functools.partial(_sgu_kernel, mix_len=mix_len, chunk=chunk),
        out_shape=jax.ShapeDtypeStruct((rows, c), F32),
        grid=(rows // tm,),
        in_specs=[
            pl.BlockSpec((tm, c), lambda r: (r, 0)),
            pl.BlockSpec((tm, c), lambda r: (r, 0)),
            pl.BlockSpec((None, N_HEADS, chunk, chunk), lambda r: (layer, 0, 0, 0)),
            pl.BlockSpec((None, chunk, c), lambda r: (layer, 0, 0)),
        ],
        out_specs=pl.BlockSpec((tm, c), lambda r: (r, 0)),
        compiler_params=_params(("parallel",)),
        name="sgu",
    )(u, v, w, b)


def _softmax_init(m_ref, l_ref, acc_ref):
    m_ref[...] = jnp.full(m_ref.shape, NEG, F32)
    l_ref[...] = jnp.zeros_like(l_ref)
    acc_ref[...] = jnp.zeros_like(acc_ref)


def _softmax_update(s, pv, m_ref, l_ref, acc_ref):
    m_prev = m_ref[...]
    m_new = jnp.maximum(m_prev, jnp.max(s, axis=-1, keepdims=True))
    alpha = jnp.exp(m_prev - m_new)
    p = jnp.exp(s - _tile_lanes(m_new, s.shape[-1] // LANES))
    l_ref[...] = alpha * l_ref[...] + jnp.sum(p, axis=-1, keepdims=True)
    acc_ref[...] = _tile_lanes(alpha, acc_ref.shape[-1] // LANES) * acc_ref[...] + pv(p.astype(BF))
    m_ref[...] = m_new


def _softmax_result(l_ref, acc_ref):
    return acc_ref[...] / _tile_lanes(l_ref[...], acc_ref.shape[-1] // LANES)


def _stacked_causal(n_stack, tq):
    row = lax.broadcasted_iota(jnp.int32, (n_stack, tq, tq), 1).reshape(n_stack * tq, tq)
    col = lax.broadcasted_iota(jnp.int32, (n_stack * tq, tq), 1)
    return col <= row


def _mla_prompt_kernel(q_ref, k_ref, v_ref, wuv_ref, o_ref, m_ref, l_ref, acc_ref, *, tq):
    qi = pl.program_id(1)
    _softmax_init(m_ref, l_ref, acc_ref)

    def tile(kj):
        start = pl.multiple_of(kj * tq, tq)
        k = k_ref[pl.ds(start, tq), :]
        s = jnp.concatenate(
            [_dot_nt(q_ref[:, h * MLA_HEAD_PAD:(h + 1) * MLA_HEAD_PAD], k[:, h * MLA_HEAD_PAD:(h + 1) * MLA_HEAD_PAD])
             for h in range(N_HEADS)], axis=0) * MLA_SCALE
        return s, v_ref[pl.ds(start, tq), :]

    def body(kj, carry):
        s, v = tile(kj)
        _softmax_update(s, lambda p: _dot(p, v), m_ref, l_ref, acc_ref)
        return carry

    lax.fori_loop(0, qi, body, 0)
    s, v = tile(qi)
    s = jnp.where(_stacked_causal(N_HEADS, tq), s, NEG)
    _softmax_update(s, lambda p: _dot(p, v), m_ref, l_ref, acc_ref)
    o_lat = _softmax_result(l_ref, acc_ref).astype(BF)
    out = jnp.zeros(o_ref.shape, F32)
    for h in range(N_HEADS):
        out = out + _dot(o_lat[h * tq:(h + 1) * tq], wuv_ref[h])
    o_ref[...] = out


def _mla_prompt(qmla, kmla, latb, layer, wuv, *, batch, seq):
    tq = min(256, seq)
    nq = seq // tq
    rows = batch * seq
    return pl.pallas_call(
        functools.partial(_mla_prompt_kernel, tq=tq),
        out_shape=jax.ShapeDtypeStruct((rows, GROUP_WIDTH), F32),
        grid=(batch, nq),
        in_specs=[
            pl.BlockSpec((tq, qmla.shape[1]), lambda b, q: (b * nq + q, 0)),
            pl.BlockSpec((seq, kmla.shape[1]), lambda b, q: (b, 0)),
            pl.BlockSpec((seq, latb.shape[1]), lambda b, q: (b, 0)),
            pl.BlockSpec((None,) + wuv.shape[1:], lambda b, q: (layer, 0, 0, 0)),
        ],
        out_specs=pl.BlockSpec((tq, GROUP_WIDTH), lambda b, q: (b * nq + q, 0)),
        scratch_shapes=[pltpu.VMEM((N_HEADS * tq, LANES), F32), pltpu.VMEM((N_HEADS * tq, LANES), F32),
                        pltpu.VMEM((N_HEADS * tq, latb.shape[1]), F32)],
        compiler_params=_params(("parallel", "arbitrary")),
        name="mla_prompt",
    )(qmla, kmla, latb, wuv)


def _topk_select(scores, n_cand_static, n_valid, topk):
    lane = lax.broadcasted_iota(jnp.int32, scores.shape, 1)
    valid = lane < n_valid
    scores = jnp.where(valid, scores, -jnp.inf)
    cnt = jnp.zeros(scores.shape, jnp.int32)
    for j in range(n_cand_static):
        cj = scores[:, j:j + 1]
        beats = (cj > scores) | ((cj == scores) & (lane > j))
        cnt = cnt + jnp.where(beats, 1, 0)
    return jnp.where(valid & (cnt < topk), 1.0, 0.0)


def _moba_prompt_kernel(q_ref, k_ref, v_ref, o_ref, qs_ref, bias_ref, m_ref, l_ref, acc_ref, *, nblk, topk):
    qi = pl.program_id(1)
    tq, c = q_ref.shape
    seq = k_ref.shape[0]
    q = q_ref[...]
    grp = _lane_group((tq, c), HEAD_DIM)
    qs_ref[...] = jnp.concatenate([jnp.where(grp == h, q, jnp.zeros_like(q)) for h in range(N_HEADS)], axis=0)
    qs = qs_ref[...]
    _softmax_init(m_ref, l_ref, acc_ref)

    brow = lax.broadcasted_iota(jnp.int32, (8, seq), 0)
    tcol = lax.broadcasted_iota(jnp.int32, (8, seq), 1)
    groups = -(-nblk // 8)
    means = [_dot(jnp.where(tcol // MOBA_BLOCK == brow + 8 * g, 1.0 / MOBA_BLOCK, 0.0).astype(BF), k_ref[...])
             for g in range(groups)]
    means = jnp.concatenate(means + [jnp.zeros((LANES - 8 * groups, c), F32)], axis=0).astype(BF)
    sel = _topk_select(_dot_nt(qs, means), nblk - 1, qi, topk)
    bias_ref[...] = ((1.0 - sel) * NEG).astype(BF)
    key_lane = lax.broadcasted_iota(jnp.int32, (tq, LANES), 1)

    start = pl.multiple_of(qi * tq, tq)
    v_own = v_ref[pl.ds(start, tq), :]
    s = jnp.where(_stacked_causal(N_HEADS, tq), _dot_nt(qs, k_ref[pl.ds(start, tq), :]), NEG)
    _softmax_update(s, lambda p: _dot(p, v_own), m_ref, l_ref, acc_ref)

    def body(j, carry):
        off = pl.multiple_of(j * tq, tq)
        v = v_ref[pl.ds(off, tq), :]
        pick = jnp.where(key_lane == j, 1.0, 0.0).astype(BF)
        s = _dot_nt(qs, k_ref[pl.ds(off, tq), :]) + _dot_nt(bias_ref[...], pick)
        _softmax_update(s, lambda p: _dot(p, v), m_ref, l_ref, acc_ref)
        return carry

    lax.fori_loop(0, qi, body, 0)
    res = _softmax_result(l_ref, acc_ref)
    out = jnp.zeros((tq, c), F32)
    for h in range(N_HEADS):
        out = jnp.where(grp == h, res[h * tq:(h + 1) * tq], out)
    o_ref[...] = out


def _moba_prompt(qd, kdb, vdb, *, batch, seq):
    tq = MOBA_BLOCK
    assert seq % tq == 0
    nblk = seq // tq
    n_cand = (seq - 1) // MOBA_BLOCK
    topk = min(MOBA_TOPK, n_cand)
    rows, c = qd.shape
    return pl.pallas_call(
        functools.partial(_moba_prompt_kernel, nblk=nblk, topk=topk),
        out_shape=jax.ShapeDtypeStruct((rows, c), F32),
        grid=(batch, nblk),
        in_specs=[
            pl.BlockSpec((tq, c), lambda b, q: (b * nblk + q, 0)),
            pl.BlockSpec((seq, c), lambda b, q: (b, 0)),
            pl.BlockSpec((seq, c), lambda b, q: (b, 0)),
        ],
        out_specs=pl.BlockSpec((tq, c), lambda b, q: (b * nblk + q, 0)),
        scratch_shapes=[pltpu.VMEM((N_HEADS * tq, c), BF), pltpu.VMEM((N_HEADS * tq, LANES), BF),
                        pltpu.VMEM((N_HEADS * tq, LANES), F32), pltpu.VMEM((N_HEADS * tq, LANES), F32),
                        pltpu.VMEM((N_HEADS * tq, c), F32)],
        compiler_params=_params(("parallel", "arbitrary")),
        name="moba_prompt",
    )(qd, kdb, vdb)


def _stack_heads(q8, width):
    grp = _lane_group(q8.shape, width)
    return jnp.concatenate([jnp.where(grp == h, q8, 0.0) for h in range(N_HEADS)], axis=0)


def _new_key_mask(rows, dec):
    r = lax.broadcasted_iota(jnp.int32, (rows, LANES), 0) % Q_PAD
    col = lax.broadcasted_iota(jnp.int32, (rows, LANES), 1)
    return (col <= r) & (col < dec)


def _pad_rows(x, rows):
    return jnp.concatenate([x, jnp.zeros((rows - x.shape[0], x.shape[1]), x.dtype)], axis=0)


def _mla_sample_kernel(pt_ref, q_ref, qn_ref, qr_ref, knew_ref, lnew_ref, wuk_ref, gk_ref, gsel_ref, wuv_ref,
                       *rest, pps, nchunks, dec):
    lat_refs, krt_refs = rest[:pps], rest[pps:2 * pps]
    o_ref, qs_ref, qabs_ref, m_ref, l_ref, acc_ref = rest[2 * pps:]
    c = pl.program_id(1)
    page = lat_refs[0].shape[0]

    @pl.when(c == 0)
    def _():
        qs_ref[...] = _stack_heads(q_ref[...], MLA_HEAD_PAD).astype(BF)
        qg = _stack_heads(qn_ref[...] * gk_ref[...], HEAD_DIM).astype(BF)
        qabs_ref[...] = _dot_nt(qg, wuk_ref[...]).astype(BF)
        _softmax_init(m_ref, l_ref, acc_ref)

    qabs = qabs_ref[...]
    qr = qr_ref[...].astype(BF)
    lats, scores = [], []
    for i in range(pps):
        lat_b = lat_refs[i][...].astype(BF)
        kn_raw = _dot(lat_b, wuk_ref[...])
        ms = _dot_nt(gsel_ref[...], (kn_raw * kn_raw).astype(BF))
        s = _dot_nt(qabs, lat_b) * lax.rsqrt(ms + EPS)
        scores.append(s + _dot(qr, krt_refs[i][...].astype(BF)))
        lats.append(lat_b)
    s = jnp.concatenate(scores, axis=-1) * MLA_SCALE

    def pv(p):
        out = _dot(p[:, :page], lats[0])
        for i in range(1, pps):
            out = out + _dot(p[:, i * page:(i + 1) * page], lats[i])
        return out

    _softmax_update(s, pv, m_ref, l_ref, acc_ref)

    @pl.when(c == nchunks - 1)
    def _():
        k_new = _pad_rows(knew_ref[...], LANES).astype(BF)
        l_new = _pad_rows(lnew_ref[...], LANES).astype(BF)
        qs = qs_ref[...]
        s_new = jnp.where(_new_key_mask(qs.shape[0], dec), _dot_nt(qs, k_new) * MLA_SCALE, NEG)
        _softmax_update(s_new, lambda p: _dot(p, l_new), m_ref, l_ref, acc_ref)
        o_lat = _softmax_result(l_ref, acc_ref).astype(BF)
        out = jnp.zeros(o_ref.shape, F32)
        for h in range(N_HEADS):
            out = out + _dot(o_lat[h * Q_PAD:(h + 1) * Q_PAD], wuv_ref[h])
        o_ref[...] = out


def _pages_per_step(n_pages):
    for pps in (16, 8, 4, 2):
        if n_pages % pps == 0:
            return pps
    raise AssertionError("page count must be even")


def _mla_sample(page_table, q8, qn8, qr8, knew8, lnew8, cache_lat, cache_krt, layer, wp, *, dec):
    db, n_pages = page_table.shape
    pps = _pages_per_step(n_pages)
    nchunks = n_pages // pps
    page, rank = cache_lat.shape[2], cache_lat.shape[3]
    rope_w = cache_krt.shape[2]
    rows = N_HEADS * Q_PAD

    def bspec(a):
        return pl.BlockSpec((None,) + a.shape[1:], lambda b, c, pt: (b, 0, 0))

    def wspec(a):
        shape = a.shape[1:]
        return pl.BlockSpec((None,) + shape, lambda b, c, pt: (layer,) + (0,) * len(shape))

    def cspec(a):
        return pl.BlockSpec(a.shape, lambda b, c, pt: (0,) * a.ndim)

    def page_spec(shape, i):
        return pl.BlockSpec((None, None) + shape, lambda b, c, pt: (layer, pt[b, c * pps + i], 0, 0))

    in_specs = ([bspec(q8), bspec(qn8), bspec(qr8), bspec(knew8), bspec(lnew8),
                 wspec(wp['w_uk256']), wspec(wp['gk256']), cspec(wp['gsel']), wspec(wp['w_uv'])]
                + [page_spec((page, rank), i) for i in range(pps)]
                + [page_spec((rope_w, page), i) for i in range(pps)])
    return pl.pallas_call(
        functools.partial(_mla_sample_kernel, pps=pps, nchunks=nchunks, dec=dec),
        out_shape=jax.ShapeDtypeStruct((db, Q_PAD, GROUP_WIDTH), F32),
        grid_spec=pltpu.PrefetchScalarGridSpec(
            num_scalar_prefetch=1,
            grid=(db, nchunks),
            in_specs=in_specs,
            out_specs=pl.BlockSpec((None, Q_PAD, GROUP_WIDTH), lambda b, c, pt: (b, 0, 0)),
            scratch_shapes=[pltpu.VMEM((rows, q8.shape[2]), BF), pltpu.VMEM((rows, rank), BF),
                            pltpu.VMEM((rows, LANES), F32), pltpu.VMEM((rows, LANES), F32),
                            pltpu.VMEM((rows, rank), F32)]),
        compiler_params=_params(("parallel", "arbitrary")),
        name="mla_sample",
    )(page_table, q8, qn8, qr8, knew8, lnew8, wp['w_uk256'], wp['gk256'], wp['gsel'], wp['w_uv'],
      *([cache_lat] * pps), *([cache_krt] * pps))


def _moba_sample_kernel(pt_ref, q_ref, knew_ref, vnew_ref, *rest, pps, nchunks, nblk, dec, topk):
    kt_refs, vt_refs = rest[:pps], rest[pps:2 * pps]
    o_ref, qs_ref, ss_ref, m_ref, l_ref, acc_ref = rest[2 * pps:]
    c = pl.program_id(1)
    ppb = 2
    bpc = pps // ppb
    page = kt_refs[0].shape[1]

    rows = qs_ref.shape[0]
    lane = lax.broadcasted_iota(jnp.int32, (rows, LANES), 1)

    @pl.when(c == 0)
    def _():
        qs_ref[...] = _stack_heads(q_ref[...], HEAD_DIM).astype(BF)
        ss_ref[...] = jnp.zeros_like(ss_ref)
        m_ref[...] = jnp.full(m_ref.shape, NEG, F32)
        l_ref[...] = jnp.zeros_like(l_ref)

    qs = qs_ref[...]
    ss_all, m_all, l_all = ss_ref[...], m_ref[...], l_ref[...]
    for bi in range(bpc):
        j = c * bpc + bi
        pages = range(ppb * bi, ppb * (bi + 1))
        s = jnp.concatenate([_dot(qs, kt_refs[g][...].astype(BF)) for g in pages], axis=-1)
        m = jnp.max(s, axis=-1, keepdims=True)
        p = jnp.exp(s - m)
        here = lane == j
        ss_all = jnp.where(here, jnp.sum(s, axis=-1, keepdims=True) * (1.0 / MOBA_BLOCK), ss_all)
        m_all = jnp.where(here, m, m_all)
        l_all = jnp.where(here, jnp.sum(p, axis=-1, keepdims=True), l_all)
        acc = jnp.zeros(acc_ref.shape[1:], F32)
        for n, g in enumerate(pages):
            acc = acc + _dot_nt(p[:, n * page:(n + 1) * page].astype(BF), vt_refs[g][...].astype(BF))
        acc_ref[j] = acc
    ss_ref[...] = ss_all
    m_ref[...] = m_all
    l_ref[...] = l_all

    @pl.when(c == nchunks - 1)
    def _():
        sel = _topk_select(ss_all, nblk, nblk, topk) > 0.0
        k_new = _pad_rows(knew_ref[...], LANES).astype(BF)
        v_new = _pad_rows(vnew_ref[...], LANES).astype(BF)
        s_own = jnp.where(_new_key_mask(rows, dec), _dot_nt(qs, k_new), NEG)
        m_tot = jnp.maximum(jnp.max(s_own, axis=-1, keepdims=True),
                            jnp.max(jnp.where(sel, m_all, NEG), axis=-1, keepdims=True))
        w = jnp.where(sel, jnp.exp(m_all - m_tot), 0.0)
        p_own = jnp.exp(s_own - m_tot)
        l_tot = jnp.sum(p_own, axis=-1, keepdims=True) + jnp.sum(w * l_all, axis=-1, keepdims=True)
        o_all = _dot(p_own.astype(BF), v_new)
        for j in range(nblk):
            o_all = o_all + w[:, j:j + 1] * acc_ref[j]
        o_all = o_all / l_tot
        grp = _lane_group(o_ref.shape, HEAD_DIM)
        out = jnp.zeros(o_ref.shape, F32)
        for h in range(N_HEADS):
            out = jnp.where(grp == h, o_all[h * Q_PAD:(h + 1) * Q_PAD], out)
        o_ref[...] = out


def _moba_sample(page_table, q8, knew8, vnew8, cache_kt, cache_vt, layer, *, dec):
    db, n_pages = page_table.shape
    pps = _pages_per_step(n_pages)
    nchunks = n_pages // pps
    c, page = cache_kt.shape[2], cache_kt.shape[3]
    assert 2 * page == MOBA_BLOCK and dec <= Q_PAD
    nblk = n_pages // 2
    assert nblk <= LANES
    topk = min(MOBA_TOPK, nblk)
    rows = N_HEADS * Q_PAD

    def bspec():
        return pl.BlockSpec((None, Q_PAD, c), lambda b, ch, pt: (b, 0, 0))

    def page_spec(i):
        return pl.BlockSpec((None, None, c, page), lambda b, ch, pt: (layer, pt[b, ch * pps + i], 0, 0))

    return pl.pallas_call(
        functools.partial(_moba_sample_kernel, pps=pps, nchunks=nchunks, nblk=nblk, dec=dec, topk=topk),
        out_shape=jax.ShapeDtypeStruct((db, Q_PAD, c), F32),
        grid_spec=pltpu.PrefetchScalarGridSpec(
            num_scalar_prefetch=1,
            grid=(db, nchunks),
            in_specs=[bspec(), bspec(), bspec()] + [page_spec(i) for i in range(pps)] * 2,
            out_specs=pl.BlockSpec((None, Q_PAD, c), lambda b, ch, pt: (b, 0, 0)),
            scratch_shapes=[pltpu.VMEM((rows, c), BF), pltpu.VMEM((rows, LANES), F32),
                            pltpu.VMEM((rows, LANES), F32), pltpu.VMEM((rows, LANES), F32),
                            pltpu.VMEM((nblk, rows, c), F32)]),
        compiler_params=_params(("parallel", "arbitrary")),
        name="moba_sample",
    )(page_table, q8, knew8, vnew8, *([cache_kt] * pps), *([cache_vt] * pps))


def _outproj_kernel(x_ref, a_ref, b_ref, c_ref, d_ref, w_ref, o_ref):
    mixed = jnp.concatenate([a_ref[...], b_ref[...], c_ref[...], d_ref[...]], axis=-1).astype(BF)
    o_ref[...] = x_ref[...] + _dot(mixed, w_ref[...])


def _outproj(x, a, b, c, d, layer, w_out):
    rows, dm = x.shape
    tm = _row_tile(rows, 512)
    gw = a.shape[1]
    return pl.pallas_call(
        _outproj_kernel,
        out_shape=jax.ShapeDtypeStruct((rows, dm), F32),
        grid=(rows // tm,),
        in_specs=[pl.BlockSpec((tm, dm), lambda r: (r, 0))]
                 + [pl.BlockSpec((tm, gw), lambda r: (r, 0))] * 4
                 + [pl.BlockSpec((None,) + w_out.shape[1:], lambda r: (layer, 0, 0))],
        out_specs=pl.BlockSpec((tm, dm), lambda r: (r, 0)),
        compiler_params=_params(("parallel",)),
        name="outproj",
    )(x, a, b, c, d, w_out)


def _ple_kernel(x_ref, pe_ref, g_ref, wg_ref, wp_ref, o_ref):
    x = x_ref[...]
    gate = jax.nn.sigmoid(_dot(_rms(x, g_ref[...]).astype(BF), wg_ref[...]))
    o_ref[...] = x + gate * _dot(pe_ref[...].astype(BF), wp_ref[...])


def _ple(x, pe, layer, g, w_gate, w_ple):
    rows, dm = x.shape
    tm = _row_tile(rows, 512)
    pd = pe.shape[-1]
    return pl.pallas_call(
        _ple_kernel,
        out_shape=jax.ShapeDtypeStruct((rows, dm), F32),
        grid=(rows // tm,),
        in_specs=[
            pl.BlockSpec((tm, dm), lambda r: (r, 0)),
            pl.BlockSpec((None, tm, pd), lambda r: (layer, r, 0)),
            pl.BlockSpec((None, 1, dm), lambda r: (layer, 0, 0)),
            pl.BlockSpec((None, dm, dm), lambda r: (layer, 0, 0)),
            pl.BlockSpec((None, pd, dm), lambda r: (layer, 0, 0)),
        ],
        out_specs=pl.BlockSpec((tm, dm), lambda r: (r, 0)),
        compiler_params=_params(("parallel",)),
        name="ple",
    )(x, pe, g, w_gate, w_ple)


def _rope_tables(pos):
    pos = pos.astype(F32)[:, None]
    n = pos.shape[0]

    def cs(half):
        inv_freq = ROPE_THETA ** (-jnp.arange(half, dtype=F32) / half)
        ang = pos * inv_freq[None, :]
        return jnp.cos(ang), jnp.sin(ang)

    c16, s16 = cs(MLA_ROPE // 2)
    one = lambda w: jnp.ones((n, w), F32)
    zero = lambda w: jnp.zeros((n, w), F32)
    cos_m = jnp.concatenate([one(64), c16, c16, one(32)], -1)
    sa_m = jnp.concatenate([zero(64), -s16, zero(48)], -1)
    sb_m = jnp.concatenate([zero(80), s16, zero(32)], -1)
    c32, s32 = cs(HEAD_DIM // 2)
    cos_d = jnp.concatenate([c32] * 4, -1)
    sa_d = jnp.concatenate([-s32, zero(32), -s32, zero(32)], -1)
    sb_d = jnp.concatenate([zero(32), s32, zero(32), s32], -1)
    return jnp.concatenate([cos_m, sa_m, sb_m, cos_d, sa_d, sb_d], -1)


def _group_mean_matrix_512():
    g = np.zeros((4 * MLA_HEAD_PAD, 4 * MLA_HEAD_PAD), np.float32)
    for h in range(N_HEADS):
        o = h * MLA_HEAD_PAD
        g[o:o + MLA_NOPE, o:o + MLA_NOPE] = 1.0 / MLA_NOPE
        g[o + MLA_NOPE:o + MLA_NOPE + MLA_ROPE, o + MLA_NOPE:o + MLA_NOPE + MLA_ROPE] = 1.0 / MLA_ROPE
    return jnp.asarray(g, BF)


def _group_mean_matrix_256():
    g = np.zeros((GROUP_WIDTH, GROUP_WIDTH), np.float32)
    for h in range(N_HEADS):
        o = h * HEAD_DIM
        g[o:o + HEAD_DIM, o:o + HEAD_DIM] = 1.0 / HEAD_DIM
    return jnp.asarray(g, BF)


def _prepare_weights(w_in, mix_norm, mla_q_norm, mla_w_uq, mla_kv_norm, mla_w_uk, mla_w_uv,
                     mla_q_nope_norm, mla_q_rope_norm, mla_k_nope_norm, mla_k_rope_norm,
                     moba_q_norm, moba_k_norm):
    nl, d, _ = w_in.shape
    zc = lambda w: jnp.zeros((nl, d, w), w_in.dtype)
    w_in_p = jnp.concatenate([w_in[..., :384], zc(_KR_LANE), w_in[..., 384:416],
                              zc(128 - _KR_LANE - MLA_ROPE), w_in[..., 416:]], axis=-1).astype(BF)
    assert w_in_p.shape[-1] == _C_END
    pad_h = lambda w: jnp.pad(w, ((0, 0), (0, 0), (0, 0), (0, MLA_HEAD_PAD - w.shape[-1])))
    w_uq = pad_h(mla_w_uq).reshape(nl, mla_w_uq.shape[1], -1).astype(BF)
    w_uk = pad_h(mla_w_uk).reshape(nl, mla_w_uk.shape[1], -1).astype(BF)
    eye = jnp.eye(N_HEADS, dtype=mla_w_uv.dtype)
    w_uv = jnp.einsum('lrhd,hg->lhrgd', mla_w_uv, eye).reshape(nl, N_HEADS, mla_w_uv.shape[1], -1).astype(BF)
    z = lambda w: jnp.zeros((nl, w), F32)
    gq = jnp.tile(jnp.concatenate([mla_q_nope_norm, mla_q_rope_norm, z(32)], -1), (1, N_HEADS))
    gk = jnp.tile(jnp.concatenate([mla_k_nope_norm, z(64)], -1), (1, N_HEADS))
    gkr = jnp.concatenate([z(_KR_LANE), mla_k_rope_norm, z(128 - _KR_LANE - MLA_ROPE)], -1)
    row = lambda a: a[:, None, :].astype(F32)
    return {
        'w_in': w_in_p, 'mix_norm': row(mix_norm), 'mla_q_norm': row(mla_q_norm), 'w_uq': w_uq,
        'gq': row(gq), 'mla_kv_norm': row(mla_kv_norm), 'w_uk': w_uk, 'gk': row(gk), 'gkr': row(gkr),
        'w_uv': w_uv, 'gm512': _group_mean_matrix_512(), 'gm256': _group_mean_matrix_256(),
        'w_uk256': mla_w_uk.reshape(nl, mla_w_uk.shape[1], -1).astype(BF),
        'gk256': row(jnp.tile(mla_k_nope_norm, (1, N_HEADS))),
        'gsel': jnp.repeat(_group_mean_matrix_256()[::HEAD_DIM], Q_PAD, axis=0),
        'dqn': row(jnp.tile(moba_q_norm, (1, N_HEADS))), 'dkn': row(jnp.tile(moba_k_norm, (1, N_HEADS))),
    }


def _head_bias(b):
    return jnp.repeat(jnp.swapaxes(b, 1, 2), HEAD_DIM, axis=-1)


def _pad_queries(a, db, dec):
    a = a.astype(F32).reshape(db, dec, a.shape[-1])
    return jnp.pad(a, ((0, 0), (0, Q_PAD - dec), (0, 0)))


def kernel(x_prompt, x_sample, cache_mla_latent, cache_mla_krope, cache_moba_k, cache_moba_v, state_pool, page_table, p_prompt, p_sample, ffn1_norm, ffn1_w_gate, ffn1_w_up, ffn1_w_down, mix_norm, w_in, mla_q_norm, mla_w_uq, mla_kv_norm, mla_w_uk, mla_w_uv, mla_q_nope_norm, mla_q_rope_norm, mla_k_nope_norm, mla_k_rope_norm, pool_w, pool_scale, sgu_w, sgu_b, moba_q_norm, moba_k_norm, w_out, ffn2_norm, ffn2_w_gate, ffn2_w_up, ffn2_w_down, ple_w, ple_gate_norm, ple_w_gate):
    batch, seq, d = x_prompt.shape
    db, dec, _ = x_sample.shape
    depth = w_in.shape[0]
    n_pages = page_table.shape[1]
    page = cache_mla_latent.shape[2]
    past_len = n_pages * page
    assert past_len % MOBA_BLOCK == 0 and seq % MOBA_BLOCK == 0 and dec <= Q_PAD
    rows_p, rows_s = batch * seq, db * dec

    wp = _prepare_weights(w_in, mix_norm, mla_q_norm, mla_w_uq, mla_kv_norm, mla_w_uk, mla_w_uv,
                          mla_q_nope_norm, mla_q_rope_norm, mla_k_nope_norm, mla_k_rope_norm,
                          moba_q_norm, moba_k_norm)
    row = lambda a: a[:, None, :].astype(F32)
    bf = lambda a: a.astype(BF)
    f1 = (row(ffn1_norm), bf(ffn1_w_gate), bf(ffn1_w_up), bf(ffn1_w_down))
    f2 = (row(ffn2_norm), bf(ffn2_w_gate), bf(ffn2_w_up), bf(ffn2_w_down))
    w_out_b, ple_w_b, ple_wg_b, ple_gn = bf(w_out), bf(ple_w), bf(ple_w_gate), row(ple_gate_norm)
    eye = jnp.eye(len(POOL_WINDOWS), dtype=pool_w.dtype)
    pool_w_bd = bf(jnp.einsum('lgcd,gh->lgchd', pool_w, eye).reshape(depth, GROUP_WIDTH, GROUP_WIDTH))
    pool_sc = row(pool_scale)

    mix_p = SGU_CHUNK if seq % SGU_CHUNK == 0 else seq
    sgu_w_p = sgu_w[:, :, :mix_p, :mix_p]
    sgu_b_p = _head_bias(sgu_b[:, :, :mix_p])
    mix_s = SGU_CHUNK if dec % SGU_CHUNK == 0 else dec
    chunk_s = mix_s * math.gcd(rows_s // mix_s, 32)
    sgu_w_s = jnp.tile(sgu_w[:, :, :mix_s, :mix_s], (1, 1, chunk_s // mix_s, chunk_s // mix_s))
    sgu_b_s = jnp.tile(_head_bias(sgu_b[:, :, :mix_s]), (1, chunk_s // mix_s, 1))

    tab_p = _rope_tables(jnp.arange(seq))
    tab_s = _rope_tables(past_len + (jnp.arange(rows_s) % dec))

    page_major = lambda a: jnp.transpose(a, (0, 1, 3, 4, 2)).reshape(a.shape[:2] + (-1, a.shape[2]))
    cache_kt, cache_vt = page_major(cache_moba_k), page_major(cache_moba_v)
    cache_krt = jnp.swapaxes(cache_mla_krope, 2, 3)
    hist_pad = -(-(POOL_HIST + dec) // 8) * 8

    xp = x_prompt.reshape(rows_p, d)
    xs = x_sample.reshape(rows_s, d)
    pe_p = p_prompt.reshape(depth, rows_p, -1)
    pe_s = p_sample.reshape(depth, rows_s, -1)
    st_p = [[] for _ in range(5)]
    st_s = [[] for _ in range(6)]
    for i in range(depth):
        xp = _ffn(xp, i, *f1)
        (qmla, kmla, lat, latb, kr, zb, u, v, qd, kd, kdb, vd, vdb) = _inproj(xp, i, wp, tab_p)
        out_a = _mla_prompt(qmla, kmla, latb, i, wp['w_uv'], batch=batch, seq=seq)
        out_b = _pool(zb, i, pool_w_bd, pool_sc, seq=seq, n_seq_blk=1, pos_start=0)
        out_c = _sgu(u, v, i, sgu_w_p, sgu_b_p, mix_len=mix_p, chunk=mix_p, tm=_row_tile(seq, 1024))
        out_d = _moba_prompt(qd, kdb, vdb, batch=batch, seq=seq)
        xp = _outproj(xp, out_a, out_b, out_c, out_d, i, w_out_b)
        xp = _ffn(xp, i, *f2)
        xp = _ple(xp, pe_p, i, ple_gn, ple_wg_b, ple_w_b)
        st_p[0].append(lat.reshape(batch, seq, -1))
        st_p[1].append(kr.reshape(batch, seq, -1))
        st_p[2].append(kd.reshape(batch, seq, N_HEADS, HEAD_DIM))
        st_p[3].append(vd.reshape(batch, seq, N_HEADS, HEAD_DIM))
        st_p[4].append(zb.reshape(batch, seq, -1)[:, seq - POOL_HIST:])

        xs = _ffn(xs, i, *f1)
        (qmla, kmla, lat, latb, kr, zb, u, v, qd, kd, kdb, vd, vdb) = _inproj(xs, i, wp, tab_s)
        q8 = _pad_queries(qmla, db, dec)
        qr8 = q8.reshape(db, Q_PAD, N_HEADS, MLA_HEAD_PAD)[..., MLA_NOPE:MLA_NOPE + MLA_ROPE]
        qr8 = jnp.swapaxes(qr8, 1, 2).reshape(db, N_HEADS * Q_PAD, MLA_ROPE)
        qn8 = q8.reshape(db, Q_PAD, N_HEADS, MLA_HEAD_PAD)[..., :MLA_NOPE].reshape(db, Q_PAD, -1)
        out_a = _mla_sample(page_table, q8, qn8, qr8, _pad_queries(kmla, db, dec), _pad_queries(latb, db, dec),
                            cache_mla_latent, cache_krt, i, wp, dec=dec)
        out_a = out_a[:, :dec].reshape(rows_s, -1)
        full = jnp.concatenate([state_pool[i], zb.reshape(db, dec, -1)], axis=1)
        full_p = jnp.pad(full, ((0, 0), (0, hist_pad - full.shape[1]), (0, 0)))
        out_b = _pool(full_p.reshape(db * hist_pad, -1), i, pool_w_bd, pool_sc, seq=hist_pad, n_seq_blk=db,
                      pos_start=past_len - POOL_HIST)
        out_b = out_b.reshape(db, hist_pad, -1)[:, POOL_HIST:POOL_HIST + dec].reshape(rows_s, -1)
        out_c = _sgu(u, v, i, sgu_w_s, sgu_b_s, mix_len=mix_s, chunk=chunk_s, tm=rows_s)
        out_d = _moba_sample(page_table, _pad_queries(qd, db, dec), _pad_queries(kdb, db, dec),
                             _pad_queries(vdb, db, dec), cache_kt, cache_vt, i, dec=dec)
        out_d = out_d[:, :dec].reshape(rows_s, -1)
        xs = _outproj(xs, out_a, out_b, out_c, out_d, i, w_out_b)
        xs = _ffn(xs, i, *f2)
        xs = _ple(xs, pe_s, i, ple_gn, ple_wg_b, ple_w_b)
        st_s[0].append(lat.reshape(db, dec, -1))
        st_s[1].append(kr.reshape(db, dec, -1))
        st_s[2].append(kd.reshape(db, dec, N_HEADS, HEAD_DIM))
        st_s[3].append(vd.reshape(db, dec, N_HEADS, HEAD_DIM))
        st_s[4].append(full[:, full.shape[1] - POOL_HIST:])
        st_s[5].append(v.reshape(db, dec, -1))

    sp = [jnp.stack(s) for s in st_p]
    ss = [jnp.stack(s) for s in st_s]
    return (xp.reshape(batch, seq, d), xs.reshape(db, dec, d), sp[0], sp[1], ss[0], ss[1],
            sp[2], sp[3], ss[2], ss[3], sp[4], ss[4], ss[5])
```

```python
import functools
import math

import numpy as np
import jax
import jax.numpy as jnp
from jax import lax
from jax.experimental import pallas as pl
from jax.experimental.pallas import tpu as pltpu

F32 = jnp.float32
BF = jnp.bfloat16

EPS = 1e-6
ROPE_THETA = 10000.0
HEAD_DIM = 64
N_HEADS = 4
GROUP_WIDTH = N_HEADS * HEAD_DIM
MLA_NOPE = HEAD_DIM
MLA_ROPE = HEAD_DIM // 2
MLA_HEAD_PAD = 128
MLA_SCALE = 1.0 / math.sqrt(MLA_NOPE + MLA_ROPE)
POOL_WINDOWS = (2, 4, 8, 16)
POOL_HIST = 15
SGU_CHUNK = 128
MOBA_BLOCK = 256
MOBA_TOPK = 3
MOBA_SCALE = 1.0 / math.sqrt(HEAD_DIM)
NEG = -1e30
LANES = 128
Q_PAD = 8
VMEM_LIMIT = 56 * 1024 * 1024


def _dot(a, b):
    return jnp.dot(a, b, preferred_element_type=F32)


def _dot_nt(a, b):
    return lax.dot_general(a, b, (((1,), (1,)), ((), ())), preferred_element_type=F32)


def _rms(x, g):
    ms = jnp.mean(x * x, axis=-1, keepdims=True)
    return x * lax.rsqrt(ms + EPS) * g


def _group_rms(x, gmat_ref, gain):
    ms = _dot((x * x).astype(BF), gmat_ref[...])
    return x * lax.rsqrt(ms + EPS) * gain


def _rope(x, cos, sin_a, sin_b, half):
    w = x.shape[-1]
    return x * cos + pltpu.roll(x, w - half, 1) * sin_a + pltpu.roll(x, half, 1) * sin_b


def _lane_group(shape, width):
    lane = lax.broadcasted_iota(jnp.int32, shape, len(shape) - 1)
    return lane // width


def _tile_lanes(t, n):
    return t if n == 1 else jnp.concatenate([t] * n, axis=-1)


def _params(sem, vmem=VMEM_LIMIT):
    return pltpu.CompilerParams(dimension_semantics=sem, vmem_limit_bytes=vmem)


def _row_tile(rows, want):
    tm = min(rows, want)
    assert rows % tm == 0, (rows, tm)
    return tm


def _ffn_kernel(x_ref, g_ref, wg_ref, wu_ref, wd_ref, o_ref, hn_ref, acc_ref, *, nf):
    f = pl.program_id(1)

    @pl.when(f == 0)
    def _():
        hn_ref[...] = _rms(x_ref[...], g_ref[...]).astype(BF)
        acc_ref[...] = jnp.zeros_like(acc_ref)

    hn = hn_ref[...]
    gate = _dot(hn, wg_ref[...])
    up = _dot(hn, wu_ref[...])
    h = (gate * jax.nn.sigmoid(gate)) * up
    acc_ref[...] += _dot(h.astype(BF), wd_ref[...])

    @pl.when(f == nf - 1)
    def _():
        o_ref[...] = x_ref[...] + 0.5 * acc_ref[...]


def _ffn(x, layer, g, wg, wu, wd):
    rows, d = x.shape
    dff = wg.shape[-1]
    tm = _row_tile(rows, 512)
    tf = dff // 2 if (dff // 2) % LANES == 0 else dff
    nf = dff // tf
    return pl.pallas_call(
        functools.partial(_ffn_kernel, nf=nf),
        out_shape=jax.ShapeDtypeStruct((rows, d), F32),
        grid=(rows // tm, nf),
        in_specs=[
            pl.BlockSpec((tm, d), lambda r, f: (r, 0)),
            pl.BlockSpec((None, 1, d), lambda r, f: (layer, 0, 0)),
            pl.BlockSpec((None, d, tf), lambda r, f: (layer, 0, f)),
            pl.BlockSpec((None, d, tf), lambda r, f: (layer, 0, f)),
            pl.BlockSpec((None, tf, d), lambda r, f: (layer, f, 0)),
        ],
        out_specs=pl.BlockSpec((tm, d), lambda r, f: (r, 0)),
        scratch_shapes=[pltpu.VMEM((tm, d), BF), pltpu.VMEM((tm, d), F32)],
        compiler_params=_params(("parallel", "arbitrary")),
        name="ffn",
    )(x, g, wg, wu, wd)


_C_CQ, _C_CKV, _C_KR, _C_ZB, _C_ZU, _C_ZV, _C_ZQ, _C_ZK, _C_ZVD, _C_END = (
    0, 256, 384, 512, 768, 1024, 1280, 1536, 1792, 2048)
_KR_LANE = 64


def _inproj_kernel(x_ref, g_ref, w_ref, qn_ref, wuq_ref, gq_ref, kvn_ref, wuk_ref, gk_ref,
                   gkr_ref, gm512_ref, gm256_ref, dqn_ref, dkn_ref, rt_ref,
                   qmla_ref, kmla_ref, lat_ref, latb_ref, kr_ref, zb_ref, u_ref, v_ref,
                   qd_ref, kd_ref, kdb_ref, vd_ref, vdb_ref):
    hn = _rms(x_ref[...], g_ref[...]).astype(BF)
    z = _dot(hn, w_ref[...])

    rt = rt_ref[...]
    cos_m, sa_m, sb_m = rt[:, 0:128], rt[:, 128:256], rt[:, 256:384]
    cos_d, sa_d, sb_d = rt[:, 384:512], rt[:, 512:640], rt[:, 640:768]

    cq = _rms(z[:, _C_CQ:_C_CKV], qn_ref[...])
    qa = _dot(cq.astype(BF), wuq_ref[...])
    qa = _group_rms(qa, gm512_ref, gq_ref[...])
    qa = _rope(qa, _tile_lanes(cos_m, N_HEADS), _tile_lanes(sa_m, N_HEADS),
               _tile_lanes(sb_m, N_HEADS), MLA_ROPE // 2)
    qmla_ref[...] = qa.astype(BF)

    lat = _rms(z[:, _C_CKV:_C_KR], kvn_ref[...])
    lat_ref[...] = lat
    lat_b = lat.astype(BF)
    latb_ref[...] = lat_b

    kr_raw = z[:, _C_KR:_C_ZB]
    ms = jnp.sum(kr_raw * kr_raw, axis=-1, keepdims=True) * (1.0 / MLA_ROPE)
    kr = kr_raw * lax.rsqrt(ms + EPS) * gkr_ref[...]
    kr = _rope(kr, cos_m, sa_m, sb_m, MLA_ROPE // 2)
    kr_ref[...] = kr[:, _KR_LANE:_KR_LANE + MLA_ROPE]

    kn = _group_rms(_dot(lat_b, wuk_ref[...]), gm512_ref, gk_ref[...])
    kmla_ref[...] = (kn + _tile_lanes(kr, N_HEADS)).astype(BF)

    zb_ref[...] = z[:, _C_ZB:_C_ZU]
    u_ref[...] = jax.nn.gelu(z[:, _C_ZU:_C_ZV])
    v_ref[...] = jax.nn.gelu(z[:, _C_ZV:_C_ZQ])

    qd = _group_rms(z[:, _C_ZQ:_C_ZK], gm256_ref, dqn_ref[...])
    qd = _rope(qd, _tile_lanes(cos_d, 2), _tile_lanes(sa_d, 2), _tile_lanes(sb_d, 2), HEAD_DIM // 2)
    qd_ref[...] = (qd * MOBA_SCALE).astype(BF)
    kd = _group_rms(z[:, _C_ZK:_C_ZVD], gm256_ref, dkn_ref[...])
    kd = _rope(kd, _tile_lanes(cos_d, 2), _tile_lanes(sa_d, 2), _tile_lanes(sb_d, 2), HEAD_DIM // 2)
    kd_ref[...] = kd
    kdb_ref[...] = kd.astype(BF)
    vd = z[:, _C_ZVD:_C_END]
    vd_ref[...] = vd
    vdb_ref[...] = vd.astype(BF)


def _inproj(x, layer, wp, rope_tab):
    rows, d = x.shape
    tm = _row_tile(rows, 512)
    n_tab = rope_tab.shape[0] // tm
    assert rope_tab.shape[0] % tm == 0

    def wspec(a):
        shape = a.shape[1:]
        return pl.BlockSpec((None,) + shape, lambda r: (layer,) + (0,) * len(shape))

    def cspec(a):
        return pl.BlockSpec(a.shape, lambda r: (0,) * a.ndim)

    def ospec(w):
        return pl.BlockSpec((tm, w), lambda r: (r, 0))

    outs = [(4 * MLA_HEAD_PAD, BF), (4 * MLA_HEAD_PAD, BF), (128, F32), (128, BF), (MLA_ROPE, F32),
            (256, F32), (256, F32), (256, F32), (256, BF), (256, F32), (256, BF), (256, F32), (256, BF)]
    return pl.pallas_call(
        _inproj_kernel,
        out_shape=[jax.ShapeDtypeStruct((rows, w), dt) for w, dt in outs],
        grid=(rows // tm,),
        in_specs=[
            pl.BlockSpec((tm, d), lambda r: (r, 0)),
            wspec(wp['mix_norm']), wspec(wp['w_in']), wspec(wp['mla_q_norm']), wspec(wp['w_uq']),
            wspec(wp['gq']), wspec(wp['mla_kv_norm']), wspec(wp['w_uk']), wspec(wp['gk']),
            wspec(wp['gkr']), cspec(wp['gm512']), cspec(wp['gm256']), wspec(wp['dqn']), wspec(wp['dkn']),
            pl.BlockSpec((tm, rope_tab.shape[1]), lambda r: (r % n_tab, 0)),
        ],
        out_specs=[ospec(w) for w, _ in outs],
        compiler_params=_params(("parallel",)),
        name="inproj",
    )(x, wp['mix_norm'], wp['w_in'], wp['mla_q_norm'], wp['w_uq'], wp['gq'], wp['mla_kv_norm'],
      wp['w_uk'], wp['gk'], wp['gkr'], wp['gm512'], wp['gm256'], wp['dqn'], wp['dkn'], rope_tab)


def _pool_kernel(x_ref, w_ref, sc_ref, o_ref, *, seq, pos_start):
    x = x_ref[...]
    rows, c = x.shape
    t = lax.broadcasted_iota(jnp.int32, (rows // seq, seq, c), 1).reshape(rows, c)

    def shifted(a, k):
        return jnp.where(t >= k, pltpu.roll(a, k, 0), 0.0)

    s2 = x + shifted(x, 1)
    s4 = s2 + shifted(s2, 2)
    s8 = s4 + shifted(s4, 4)
    s16 = s8 + shifted(s8, 8)
    grp = _lane_group((rows, c), c // len(POOL_WINDOWS))
    win = jnp.where(grp == 0, s2, jnp.where(grp == 1, s4, jnp.where(grp == 2, s8, s16)))
    wsize = jnp.where(grp == 0, POOL_WINDOWS[0], jnp.where(grp == 1, POOL_WINDOWS[1],
                      jnp.where(grp == 2, POOL_WINDOWS[2], POOL_WINDOWS[3])))
    count = jnp.maximum(jnp.minimum(t + (pos_start + 1), wsize), 1).astype(F32)
    pooled = win / count - x
    o_ref[...] = _dot(pooled.astype(BF), w_ref[...]) * sc_ref[...]


def _pool(zb, layer, w_bd, scale, *, seq, n_seq_blk, pos_start):
    rows, c = zb.shape
    tm = seq * n_seq_blk
    assert rows % tm == 0
    return pl.pallas_call(
        functools.partial(_pool_kernel, seq=seq, pos_start=pos_start),
        out_shape=jax.ShapeDtypeStruct((rows, c), F32),
        grid=(rows // tm,),
        in_specs=[
            pl.BlockSpec((tm, c), lambda r: (r, 0)),
            pl.BlockSpec((None, c, c), lambda r: (layer, 0, 0)),
            pl.BlockSpec((None, 1, c), lambda r: (layer, 0, 0)),
        ],
        out_specs=pl.BlockSpec((tm, c), lambda r: (r, 0)),
        compiler_params=_params(("parallel",)),
        name="pool",
    )(zb, w_bd, scale)


def _sgu_kernel(u_ref, v_ref, w_ref, b_ref, o_ref, *, mix_len, chunk):
    tm, c = u_ref.shape
    r = lax.broadcasted_iota(jnp.int32, (chunk, chunk), 0)
    col = lax.broadcasted_iota(jnp.int32, (chunk, chunk), 1)
    keep = (r // mix_len == col // mix_len) & (col <= r)
    grp = _lane_group((chunk, c), HEAD_DIM)
    w_heads = [jnp.where(keep, w_ref[h], 0.0).astype(BF) for h in range(N_HEADS)]
    bias = b_ref[...]
    for ch in range(tm // chunk):
        rs = slice(ch * chunk, (ch + 1) * chunk)
        v = v_ref[rs, :].astype(BF)
        mixed = jnp.zeros((chunk, c), F32)
        for h in range(N_HEADS):
            mixed = jnp.where(grp == h, _dot(w_heads[h], v), mixed)
        o_ref[rs, :] = u_ref[rs, :] * (mixed + bias)


def _sgu(u, v, layer, w, b, *, mix_len, chunk, tm):
    rows, c = u.shape
    assert rows % tm == 0 and tm % chunk == 0
    return pl.pallas_call(
        functools.partial(_sgu_kernel, mix_len=mix_len, chunk=chunk),
        out_shape=jax.ShapeDtypeStruct((rows, c), F32),
        grid=(rows // tm,),
        in_specs=[
            pl.BlockSpec((tm, c), lambda r: (r, 0)),
            pl.BlockSpec((tm, c), lambda r: (r, 0)),
            pl.BlockSpec((None, N_HEADS, chunk, chunk), lambda r: (layer, 0, 0, 0)),
            pl.BlockSpec((None, chunk, c), lambda r: (layer, 0, 0)),
        ],
        out_specs=pl.BlockSpec((tm, c), lambda r: (r, 0)),
        compiler_params=_params(("parallel",)),
        name="sgu",
    )(u, v, w, b)


def _softmax_init(m_ref, l_ref, acc_ref):
    m_ref[...] = jnp.full(m_ref.shape, NEG, F32)
    l_ref[...] = jnp.zeros_like(l_ref)
    acc_ref[...] = jnp.zeros_like(acc_ref)


def _softmax_update(s, pv, m_ref, l_ref, acc_ref):
    m_prev = m_ref[...]
    m_new = jnp.maximum(m_prev, jnp.max(s, axis=-1, keepdims=True))
    alpha = jnp.exp(m_prev - m_new)
    p = jnp.exp(s - _tile_lanes(m_new, s.shape[-1] // LANES))
    l_ref[...] = alpha * l_ref[...] + jnp.sum(p, axis=-1, keepdims=True)
    acc_ref[...] = _tile_lanes(alpha, acc_ref.shape[-1] // LANES) * acc_ref[...] + pv(p.astype(BF))
    m_ref[...] = m_new


def _softmax_result(l_ref, acc_ref):
    return acc_ref[...] / _tile_lanes(l_ref[...], acc_ref.shape[-1] // LANES)


def _stacked_causal(n_stack, tq):
    row = lax.broadcasted_iota(jnp.int32, (n_stack, tq, tq), 1).reshape(n_stack * tq, tq)
    col = lax.broadcasted_iota(jnp.int32, (n_stack * tq, tq), 1)
    return col <= row


def _mla_prompt_kernel(q_ref, k_ref, v_ref, wuv_ref, o_ref, m_ref, l_ref, acc_ref, *, tq):
    qi = pl.program_id(1)
    _softmax_init(m_ref, l_ref, acc_ref)

    def tile(kj):
        start = pl.multiple_of(kj * tq, tq)
        k = k_ref[pl.ds(start, tq), :]
        s = jnp.concatenate(
            [_dot_nt(q_ref[:, h * MLA_HEAD_PAD:(h + 1) * MLA_HEAD_PAD], k[:, h * MLA_HEAD_PAD:(h + 1) * MLA_HEAD_PAD])
             for h in range(N_HEADS)], axis=0) * MLA_SCALE
        return s, v_ref[pl.ds(start, tq), :]

    def body(kj, carry):
        s, v = tile(kj)
        _softmax_update(s, lambda p: _dot(p, v), m_ref, l_ref, acc_ref)
        return carry

    lax.fori_loop(0, qi, body, 0)
    s, v = tile(qi)
    s = jnp.where(_stacked_causal(N_HEADS, tq), s, NEG)
    _softmax_update(s, lambda p: _dot(p, v), m_ref, l_ref, acc_ref)
    o_lat = _softmax_result(l_ref, acc_ref).astype(BF)
    out = jnp.zeros(o_ref.shape, F32)
    for h in range(N_HEADS):
        out = out + _dot(o_lat[h * tq:(h + 1) * tq], wuv_ref[h])
    o_ref[...] = out


def _mla_prompt(qmla, kmla, latb, layer, wuv, *, batch, seq):
    tq = min(256, seq)
    nq = seq // tq
    rows = batch * seq
    return pl.pallas_call(
        functools.partial(_mla_prompt_kernel, tq=tq),
        out_shape=jax.ShapeDtypeStruct((rows, GROUP_WIDTH), F32),
        grid=(batch, nq),
        in_specs=[
            pl.BlockSpec((tq, qmla.shape[1]), lambda b, q: (b * nq + q, 0)),
            pl.BlockSpec((seq, kmla.shape[1]), lambda b, q: (b, 0)),
            pl.BlockSpec((seq, latb.shape[1]), lambda b, q: (b, 0)),
            pl.BlockSpec((None,) + wuv.shape[1:], lambda b, q: (layer, 0, 0, 0)),
        ],
        out_specs=pl.BlockSpec((tq, GROUP_WIDTH), lambda b, q: (b * nq + q, 0)),
        scratch_shapes=[pltpu.VMEM((N_HEADS * tq, LANES), F32), pltpu.VMEM((N_HEADS * tq, LANES), F32),
                        pltpu.VMEM((N_HEADS * tq, latb.shape[1]), F32)],
        compiler_params=_params(("parallel", "arbitrary")),
        name="mla_prompt",
    )(qmla, kmla, latb, wuv)


def _topk_select(scores, n_cand_static, n_valid, topk):
    lane = lax.broadcasted_iota(jnp.int32, scores.shape, 1)
    valid = lane < n_valid
    scores = jnp.where(valid, scores, -jnp.inf)
    cnt = jnp.zeros(scores.shape, jnp.int32)
    for j in range(n_cand_static):
        cj = scores[:, j:j + 1]
        beats = (cj > scores) | ((cj == scores) & (lane > j))
        cnt = cnt + jnp.where(beats, 1, 0)
    return jnp.where(valid & (cnt < topk), 1.0, 0.0)


def _moba_prompt_kernel(q_ref, k_ref, v_ref, o_ref, qs_ref, bias_ref, m_ref, l_ref, acc_ref, *, nblk, topk):
    qi = pl.program_id(1)
    tq, c = q_ref.shape
    seq = k_ref.shape[0]
    q = q_ref[...]
    grp = _lane_group((tq, c), HEAD_DIM)
    qs_ref[...] = jnp.concatenate([jnp.where(grp == h, q, jnp.zeros_like(q)) for h in range(N_HEADS)], axis=0)
    qs = qs_ref[...]
    _softmax_init(m_ref, l_ref, acc_ref)

    brow = lax.broadcasted_iota(jnp.int32, (8, seq), 0)
    tcol = lax.broadcasted_iota(jnp.int32, (8, seq), 1)
    groups = -(-nblk // 8)
    means = [_dot(jnp.where(tcol // MOBA_BLOCK == brow + 8 * g, 1.0 / MOBA_BLOCK, 0.0).astype(BF), k_ref[...])
             for g in range(groups)]
    means = jnp.concatenate(means + [jnp.zeros((LANES - 8 * groups, c), F32)], axis=0).astype(BF)
    sel = _topk_select(_dot_nt(qs, means), nblk - 1, qi, topk)
    bias_ref[...] = ((1.0 - sel) * NEG).astype(BF)
    key_lane = lax.broadcasted_iota(jnp.int32, (tq, LANES), 1)

    start = pl.multiple_of(qi * tq, tq)
    v_own = v_ref[pl.ds(start, tq), :]
    s = jnp.where(_stacked_causal(N_HEADS, tq), _dot_nt(qs, k_ref[pl.ds(start, tq), :]), NEG)
    _softmax_update(s, lambda p: _dot(p, v_own), m_ref, l_ref, acc_ref)

    def body(j, carry):
        off = pl.multiple_of(j * tq, tq)
        v = v_ref[pl.ds(off, tq), :]
        pick = jnp.where(key_lane == j, 1.0, 0.0).astype(BF)
        s = _dot_nt(qs, k_ref[pl.ds(off, tq), :]) + _dot_nt(bias_ref[...], pick)
        _softmax_update(s, lambda p: _dot(p, v), m_ref, l_ref, acc_ref)
        return carry

    lax.fori_loop(0, qi, body, 0)
    res = _softmax_result(l_ref, acc_ref)
    out = jnp.zeros((tq, c), F32)
    for h in range(N_HEADS):
        out = jnp.where(grp == h, res[h * tq:(h + 1) * tq], out)
    o_ref[...] = out


def _moba_prompt(qd, kdb, vdb, *, batch, seq):
    tq = MOBA_BLOCK
    assert seq % tq == 0
    nblk = seq // tq
    n_cand = (seq - 1) // MOBA_BLOCK
    topk = min(MOBA_TOPK, n_cand)
    rows, c = qd.shape
    return pl.pallas_call(
        functools.partial(_moba_prompt_kernel, nblk=nblk, topk=topk),
        out_shape=jax.ShapeDtypeStruct((rows, c), F32),
        grid=(batch, nblk),
        in_specs=[
            pl.BlockSpec((tq, c), lambda b, q: (b * nblk + q, 0)),
            pl.BlockSpec((seq, c), lambda b, q: (b, 0)),
            pl.BlockSpec((seq, c), lambda b, q: (b, 0)),
        ],
        out_specs=pl.BlockSpec((tq, c), lambda b, q: (b * nblk + q, 0)),
        scratch_shapes=[pltpu.VMEM((N_HEADS * tq, c), BF), pltpu.VMEM((N_HEADS * tq, LANES), BF),
                        pltpu.VMEM((N_HEADS * tq, LANES), F32), pltpu.VMEM((N_HEADS * tq, LANES), F32),
                        pltpu.VMEM((N_HEADS * tq, c), F32)],
        compiler_params=_params(("parallel", "arbitrary")),
        name="moba_prompt",
    )(qd, kdb, vdb)


def _stack_heads(q8, width):
    grp = _lane_group(q8.shape, width)
    return jnp.concatenate([jnp.where(grp == h, q8, 0.0) for h in range(N_HEADS)], axis=0)


def _new_key_mask(rows, dec):
    r = lax.broadcasted_iota(jnp.int32, (rows, LANES), 0) % Q_PAD
    col = lax.broadcasted_iota(jnp.int32, (rows, LANES), 1)
    return (col <= r) & (col < dec)


def _pad_rows(x, rows):
    return jnp.concatenate([x, jnp.zeros((rows - x.shape[0], x.shape[1]), x.dtype)], axis=0)


def _mla_sample_kernel(pt_ref, q_ref, qn_ref, qr_ref, knew_ref, lnew_ref, wuk_ref, gk_ref, gsel_ref, wuv_ref,
                       *rest, pps, nchunks, dec):
    lat_refs, krt_refs = rest[:pps], rest[pps:2 * pps]
    o_ref, qs_ref, qabs_ref, m_ref, l_ref, acc_ref = rest[2 * pps:]
    c = pl.program_id(1)
    page = lat_refs[0].shape[0]

    @pl.when(c == 0)
    def _():
        qs_ref[...] = _stack_heads(q_ref[...], MLA_HEAD_PAD).astype(BF)
        qg = _stack_heads(qn_ref[...] * gk_ref[...], HEAD_DIM).astype(BF)
        qabs_ref[...] = _dot_nt(qg, wuk_ref[...]).astype(BF)
        _softmax_init(m_ref, l_ref, acc_ref)

    qabs = qabs_ref[...]
    qr = qr_ref[...].astype(BF)
    lats = [lat_refs[i][...].astype(BF) for i in range(pps)]
    kn_sq = [_dot(lat_b, wuk_ref[...]) for lat_b in lats]
    kn_sq = [(k * k).astype(BF) for k in kn_sq]
    s_nope = [_dot_nt(qabs, lat_b) for lat_b in lats]
    s_rope = [_dot(qr, krt_refs[i][...].astype(BF)) for i in range(pps)]
    ms = [_dot_nt(gsel_ref[...], k) for k in kn_sq]
    s = jnp.concatenate([s_nope[i] * lax.rsqrt(ms[i] + EPS) + s_rope[i] for i in range(pps)], axis=-1) * MLA_SCALE

    def pv(p):
        out = _dot(p[:, :page], lats[0])
        for i in range(1, pps):
            out = out + _dot(p[:, i * page:(i + 1) * page], lats[i])
        return out

    _softmax_update(s, pv, m_ref, l_ref, acc_ref)

    @pl.when(c == nchunks - 1)
    def _():
        k_new = _pad_rows(knew_ref[...], LANES).astype(BF)
        l_new = _pad_rows(lnew_ref[...], LANES).astype(BF)
        qs = qs_ref[...]
        s_new = jnp.where(_new_key_mask(qs.shape[0], dec), _dot_nt(qs, k_new) * MLA_SCALE, NEG)
        _softmax_update(s_new, lambda p: _dot(p, l_new), m_ref, l_ref, acc_ref)
        o_lat = _softmax_result(l_ref, acc_ref).astype(BF)
        out = jnp.zeros(o_ref.shape, F32)
        for h in range(N_HEADS):
            out = out + _dot(o_lat[h * Q_PAD:(h + 1) * Q_PAD], wuv_ref[h])
        o_ref[...] = out


def _pages_per_step(n_pages):
    for pps in (16, 8, 4, 2):
        if n_pages % pps == 0:
            return pps
    raise AssertionError("page count must be even")


def _mla_sample(page_table, q8, qn8, qr8, knew8, lnew8, cache_lat, cache_krt, layer, wp, *, dec):
    db, n_pages = page_table.shape
    pps = _pages_per_step(n_pages)
    nchunks = n_pages // pps
    page, rank = cache_lat.shape[2], cache_lat.shape[3]
    rope_w = cache_krt.shape[2]
    rows = N_HEADS * Q_PAD

    def bspec(a):
        return pl.BlockSpec((None,) + a.shape[1:], lambda b, c, pt: (b, 0, 0))

    def wspec(a):
        shape = a.shape[1:]
        return pl.BlockSpec((None,) + shape, lambda b, c, pt: (layer,) + (0,) * len(shape))

    def cspec(a):
        return pl.BlockSpec(a.shape, lambda b, c, pt: (0,) * a.ndim)

    def page_spec(shape, i):
        return pl.BlockSpec((None, None) + shape, lambda b, c, pt: (layer, pt[b, c * pps + i], 0, 0))

    in_specs = ([bspec(q8), bspec(qn8), bspec(qr8), bspec(knew8), bspec(lnew8),
                 wspec(wp['w_uk256']), wspec(wp['gk256']), cspec(wp['gsel']), wspec(wp['w_uv'])]
                + [page_spec((page, rank), i) for i in range(pps)]
                + [page_spec((rope_w, page), i) for i in range(pps)])
    return pl.pallas_call(
        functools.partial(_mla_sample_kernel, pps=pps, nchunks=nchunks, dec=dec),
        out_shape=jax.ShapeDtypeStruct((db, Q_PAD, GROUP_WIDTH), F32),
        grid_spec=pltpu.PrefetchScalarGridSpec(
            num_scalar_prefetch=1,
            grid=(db, nchunks),
            in_specs=in_specs,
            out_specs=pl.BlockSpec((None, Q_PAD, GROUP_WIDTH), lambda b, c, pt: (b, 0, 0)),
            scratch_shapes=[pltpu.VMEM((rows, q8.shape[2]), BF), pltpu.VMEM((rows, rank), BF),
                            pltpu.VMEM((rows, LANES), F32), pltpu.VMEM((rows, LANES), F32),
                            pltpu.VMEM((rows, rank), F32)]),
        compiler_params=_params(("parallel", "arbitrary")),
        name="mla_sample",
    )(page_table, q8, qn8, qr8, knew8, lnew8, wp['w_uk256'], wp['gk256'], wp['gsel'], wp['w_uv'],
      *([cache_lat] * pps), *([cache_krt] * pps))


def _moba_sample_kernel(pt_ref, q_ref, knew_ref, vnew_ref, *rest, pps, nchunks, nblk, dec, topk):
    kt_refs, vt_refs = rest[:pps], rest[pps:2 * pps]
    o_ref, qs_ref, ss_ref, m_ref, l_ref, acc_ref = rest[2 * pps:]
    c = pl.program_id(1)
    ppb = 2
    bpc = pps // ppb
    page = kt_refs[0].shape[1]

    rows = qs_ref.shape[0]
    lane = lax.broadcasted_iota(jnp.int32, (rows, LANES), 1)

    @pl.when(c == 0)
    def _():
        qs_ref[...] = _stack_heads(q_ref[...], HEAD_DIM).astype(BF)
        ss_ref[...] = jnp.zeros_like(ss_ref)
        m_ref[...] = jnp.full(m_ref.shape, NEG, F32)
        l_ref[...] = jnp.zeros_like(l_ref)

    qs = qs_ref[...]
    ss_all, m_all, l_all = ss_ref[...], m_ref[...], l_ref[...]
    s_pages = [_dot(qs, kt_refs[g][...].astype(BF)) for g in range(pps)]
    probs = []
    for bi in range(bpc):
        j = c * bpc + bi
        s = jnp.concatenate(s_pages[ppb * bi:ppb * (bi + 1)], axis=-1)
        m = jnp.max(s, axis=-1, keepdims=True)
        p = jnp.exp(s - m)
        probs.append(p.astype(BF))
        here = lane == j
        ss_all = jnp.where(here, jnp.sum(s, axis=-1, keepdims=True) * (1.0 / MOBA_BLOCK), ss_all)
        m_all = jnp.where(here, m, m_all)
        l_all = jnp.where(here, jnp.sum(p, axis=-1, keepdims=True), l_all)
    for bi in range(bpc):
        acc = _dot_nt(probs[bi][:, :page], vt_refs[ppb * bi][...].astype(BF))
        for n in range(1, ppb):
            acc = acc + _dot_nt(probs[bi][:, n * page:(n + 1) * page], vt_refs[ppb * bi + n][...].astype(BF))
        acc_ref[c * bpc + bi] = acc
    ss_ref[...] = ss_all
    m_ref[...] = m_all
    l_ref[...] = l_all

    @pl.when(c == nchunks - 1)
    def _():
        sel = _topk_select(ss_all, nblk, nblk, topk) > 0.0
        k_new = _pad_rows(knew_ref[...], LANES).astype(BF)
        v_new = _pad_rows(vnew_ref[...], LANES).astype(BF)
        s_own = jnp.where(_new_key_mask(rows, dec), _dot_nt(qs, k_new), NEG)
        m_tot = jnp.maximum(jnp.max(s_own, axis=-1, keepdims=True),
                            jnp.max(jnp.where(sel, m_all, NEG), axis=-1, keepdims=True))
        w = jnp.where(sel, jnp.exp(m_all - m_tot), 0.0)
        p_own = jnp.exp(s_own - m_tot)
        l_tot = jnp.sum(p_own, axis=-1, keepdims=True) + jnp.sum(w * l_all, axis=-1, keepdims=True)
        o_all = _dot(p_own.astype(BF), v_new)
        for j in range(nblk):
            o_all = o_all + w[:, j:j + 1] * acc_ref[j]
        o_all = o_all / l_tot
        grp = _lane_group(o_ref.shape, HEAD_DIM)
        out = jnp.zeros(o_ref.shape, F32)
        for h in range(N_HEADS):
            out = jnp.where(grp == h, o_all[h * Q_PAD:(h + 1) * Q_PAD], out)
        o_ref[...] = out


def _moba_sample(page_table, q8, knew8, vnew8, cache_kt, cache_vt, layer, *, dec):
    db, n_pages = page_table.shape
    pps = _pages_per_step(n_pages)
    nchunks = n_pages // pps
    c, page = cache_kt.shape[2], cache_kt.shape[3]
    assert 2 * page == MOBA_BLOCK and dec <= Q_PAD
    nblk = n_pages // 2
    assert nblk <= LANES
    topk = min(MOBA_TOPK, nblk)
    rows = N_HEADS * Q_PAD

    def bspec():
        return pl.BlockSpec((None, Q_PAD, c), lambda b, ch, pt: (b, 0, 0))

    def page_spec(i):
        return pl.BlockSpec((None, None, c, page), lambda b, ch, pt: (layer, pt[b, ch * pps + i], 0, 0))

    return pl.pallas_call(
        functools.partial(_moba_sample_kernel, pps=pps, nchunks=nchunks, nblk=nblk, dec=dec, topk=topk),
        out_shape=jax.ShapeDtypeStruct((db, Q_PAD, c), F32),
        grid_spec=pltpu.PrefetchScalarGridSpec(
            num_scalar_prefetch=1,
            grid=(db, nchunks),
            in_specs=[bspec(), bspec(), bspec()] + [page_spec(i) for i in range(pps)] * 2,
            out_specs=pl.BlockSpec((None, Q_PAD, c), lambda b, ch, pt: (b, 0, 0)),
            scratch_shapes=[pltpu.VMEM((rows, c), BF), pltpu.VMEM((rows, LANES), F32),
                            pltpu.VMEM((rows, LANES), F32), pltpu.VMEM((rows, LANES), F32),
                            pltpu.VMEM((nblk, rows, c), F32)]),
        compiler_params=_params(("parallel", "arbitrary")),
        name="moba_sample",
    )(page_table, q8, knew8, vnew8, *([cache_kt] * pps), *([cache_vt] * pps))


def _outproj_kernel(x_ref, a_ref, b_ref, c_ref, d_ref, w_ref, o_ref):
    mixed = jnp.concatenate([a_ref[...], b_ref[...], c_ref[...], d_ref[...]], axis=-1).astype(BF)
    o_ref[...] = x_ref[...] + _dot(mixed, w_ref[...])


def _outproj(x, a, b, c, d, layer, w_out):
    rows, dm = x.shape
    tm = _row_tile(rows, 512)
    gw = a.shape[1]
    return pl.pallas_call(
        _outproj_kernel,
        out_shape=jax.ShapeDtypeStruct((rows, dm), F32),
        grid=(rows // tm,),
        in_specs=[pl.BlockSpec((tm, dm), lambda r: (r, 0))]
                 + [pl.BlockSpec((tm, gw), lambda r: (r, 0))] * 4
                 + [pl.BlockSpec((None,) + w_out.shape[1:], lambda r: (layer, 0, 0))],
        out_specs=pl.BlockSpec((tm, dm), lambda r: (r, 0)),
        compiler_params=_params(("parallel",)),
        name="outproj",
    )(x, a, b, c, d, w_out)


def _ple_kernel(x_ref, pe_ref, g_ref, wg_ref, wp_ref, o_ref):
    x = x_ref[...]
    gate = jax.nn.sigmoid(_dot(_rms(x, g_ref[...]).astype(BF), wg_ref[...]))
    o_ref[...] = x + gate * _dot(pe_ref[...].astype(BF), wp_ref[...])


def _ple(x, pe, layer, g, w_gate, w_ple):
    rows, dm = x.shape
    tm = _row_tile(rows, 512)
    pd = pe.shape[-1]
    return pl.pallas_call(
        _ple_kernel,
        out_shape=jax.ShapeDtypeStruct((rows, dm), F32),
        grid=(rows // tm,),
        in_specs=[
            pl.BlockSpec((tm, dm), lambda r: (r, 0)),
            pl.BlockSpec((None, tm, pd), lambda r: (layer, r, 0)),
            pl.BlockSpec((None, 1, dm), lambda r: (layer, 0, 0)),
            pl.BlockSpec((None, dm, dm), lambda r: (layer, 0, 0)),
            pl.BlockSpec((None, pd, dm), lambda r: (layer, 0, 0)),
        ],
        out_specs=pl.BlockSpec((tm, dm), lambda r: (r, 0)),
        compiler_params=_params(("parallel",)),
        name="ple",
    )(x, pe, g, w_gate, w_ple)


def _rope_tables(pos):
    pos = pos.astype(F32)[:, None]
    n = pos.shape[0]

    def cs(half):
        inv_freq = ROPE_THETA ** (-jnp.arange(half, dtype=F32) / half)
        ang = pos * inv_freq[None, :]
        return jnp.cos(ang), jnp.sin(ang)

    c16, s16 = cs(MLA_ROPE // 2)
    one = lambda w: jnp.ones((n, w), F32)
    zero = lambda w: jnp.zeros((n, w), F32)
    cos_m = jnp.concatenate([one(64), c16, c16, one(32)], -1)
    sa_m = jnp.concatenate([zero(64), -s16, zero(48)], -1)
    sb_m = jnp.concatenate([zero(80), s16, zero(32)], -1)
    c32, s32 = cs(HEAD_DIM // 2)
    cos_d = jnp.concatenate([c32] * 4, -1)
    sa_d = jnp.concatenate([-s32, zero(32), -s32, zero(32)], -1)
    sb_d = jnp.concatenate([zero(32), s32, zero(32), s32], -1)
    return jnp.concatenate([cos_m, sa_m, sb_m, cos_d, sa_d, sb_d], -1)


def _group_mean_matrix_512():
    g = np.zeros((4 * MLA_HEAD_PAD, 4 * MLA_HEAD_PAD), np.float32)
    for h in range(N_HEADS):
        o = h * MLA_HEAD_PAD
        g[o:o + MLA_NOPE, o:o + MLA_NOPE] = 1.0 / MLA_NOPE
        g[o + MLA_NOPE:o + MLA_NOPE + MLA_ROPE, o + MLA_NOPE:o + MLA_NOPE + MLA_ROPE] = 1.0 / MLA_ROPE
    return jnp.asarray(g, BF)


def _group_mean_matrix_256():
    g = np.zeros((GROUP_WIDTH, GROUP_WIDTH), np.float32)
    for h in range(N_HEADS):
        o = h * HEAD_DIM
        g[o:o + HEAD_DIM, o:o + HEAD_DIM] = 1.0 / HEAD_DIM
    return jnp.asarray(g, BF)


def _prepare_weights(w_in, mix_norm, mla_q_norm, mla_w_uq, mla_kv_norm, mla_w_uk, mla_w_uv,
                     mla_q_nope_norm, mla_q_rope_norm, mla_k_nope_norm, mla_k_rope_norm,
                     moba_q_norm, moba_k_norm):
    nl, d, _ = w_in.shape
    zc = lambda w: jnp.zeros((nl, d, w), w_in.dtype)
    w_in_p = jnp.concatenate([w_in[..., :384], zc(_KR_LANE), w_in[..., 384:416],
                              zc(128 - _KR_LANE - MLA_ROPE), w_in[..., 416:]], axis=-1).astype(BF)
    assert w_in_p.shape[-1] == _C_END
    pad_h = lambda w: jnp.pad(w, ((0, 0), (0, 0), (0, 0), (0, MLA_HEAD_PAD - w.shape[-1])))
    w_uq = pad_h(mla_w_uq).reshape(nl, mla_w_uq.shape[1], -1).astype(BF)
    w_uk = pad_h(mla_w_uk).reshape(nl, mla_w_uk.shape[1], -1).astype(BF)
    eye = jnp.eye(N_HEADS, dtype=mla_w_uv.dtype)
    w_uv = jnp.einsum('lrhd,hg->lhrgd', mla_w_uv, eye).reshape(nl, N_HEADS, mla_w_uv.shape[1], -1).astype(BF)
    z = lambda w: jnp.zeros((nl, w), F32)
    gq = jnp.tile(jnp.concatenate([mla_q_nope_norm, mla_q_rope_norm, z(32)], -1), (1, N_HEADS))
    gk = jnp.tile(jnp.concatenate([mla_k_nope_norm, z(64)], -1), (1, N_HEADS))
    gkr = jnp.concatenate([z(_KR_LANE), mla_k_rope_norm, z(128 - _KR_LANE - MLA_ROPE)], -1)
    row = lambda a: a[:, None, :].astype(F32)
    return {
        'w_in': w_in_p, 'mix_norm': row(mix_norm), 'mla_q_norm': row(mla_q_norm), 'w_uq': w_uq,
        'gq': row(gq), 'mla_kv_norm': row(mla_kv_norm), 'w_uk': w_uk, 'gk': row(gk), 'gkr': row(gkr),
        'w_uv': w_uv, 'gm512': _group_mean_matrix_512(), 'gm256': _group_mean_matrix_256(),
        'w_uk256': mla_w_uk.reshape(nl, mla_w_uk.shape[1], -1).astype(BF),
        'gk256': row(jnp.tile(mla_k_nope_norm, (1, N_HEADS))),
        'gsel': jnp.repeat(_group_mean_matrix_256()[::HEAD_DIM], Q_PAD, axis=0),
        'dqn': row(jnp.tile(moba_q_norm, (1, N_HEADS))), 'dkn': row(jnp.tile(moba_k_norm, (1, N_HEADS))),
    }


def _head_bias(b):
    return jnp.repeat(jnp.swapaxes(b, 1, 2), HEAD_DIM, axis=-1)


def _pad_queries(a, db, dec):
    a = a.astype(F32).reshape(db, dec, a.shape[-1])
    return jnp.pad(a, ((0, 0), (0, Q_PAD - dec), (0, 0)))


def kernel(x_prompt, x_sample, cache_mla_latent, cache_mla_krope, cache_moba_k, cache_moba_v, state_pool, page_table, p_prompt, p_sample, ffn1_norm, ffn1_w_gate, ffn1_w_up, ffn1_w_down, mix_norm, w_in, mla_q_norm, mla_w_uq, mla_kv_norm, mla_w_uk, mla_w_uv, mla_q_nope_norm, mla_q_rope_norm, mla_k_nope_norm, mla_k_rope_norm, pool_w, pool_scale, sgu_w, sgu_b, moba_q_norm, moba_k_norm, w_out, ffn2_norm, ffn2_w_gate, ffn2_w_up, ffn2_w_down, ple_w, ple_gate_norm, ple_w_gate):
    batch, seq, d = x_prompt.shape
    db, dec, _ = x_sample.shape
    depth = w_in.shape[0]
    n_pages = page_table.shape[1]
    page = cache_mla_latent.shape[2]
    past_len = n_pages * page
    assert past_len % MOBA_BLOCK == 0 and seq % MOBA_BLOCK == 0 and dec <= Q_PAD
    rows_p, rows_s = batch * seq, db * dec

    wp = _prepare_weights(w_in, mix_norm, mla_q_norm, mla_w_uq, mla_kv_norm, mla_w_uk, mla_w_uv,
                          mla_q_nope_norm, mla_q_rope_norm, mla_k_nope_norm, mla_k_rope_norm,
                          moba_q_norm, moba_k_norm)
    row = lambda a: a[:, None, :].astype(F32)
    bf = lambda a: a.astype(BF)
    f1 = (row(ffn1_norm), bf(ffn1_w_gate), bf(ffn1_w_up), bf(ffn1_w_down))
    f2 = (row(ffn2_norm), bf(ffn2_w_gate), bf(ffn2_w_up), bf(ffn2_w_down))
    w_out_b, ple_w_b, ple_wg_b, ple_gn = bf(w_out), bf(ple_w), bf(ple_w_gate), row(ple_gate_norm)
    eye = jnp.eye(len(POOL_WINDOWS), dtype=pool_w.dtype)
    pool_w_bd = bf(jnp.einsum('lgcd,gh->lgchd', pool_w, eye).reshape(depth, GROUP_WIDTH, GROUP_WIDTH))
    pool_sc = row(pool_scale)

    mix_p = SGU_CHUNK if seq % SGU_CHUNK == 0 else seq
    sgu_w_p = sgu_w[:, :, :mix_p, :mix_p]
    sgu_b_p = _head_bias(sgu_b[:, :, :mix_p])
    mix_s = SGU_CHUNK if dec % SGU_CHUNK == 0 else dec
    chunk_s = mix_s * math.gcd(rows_s // mix_s, 32)
    sgu_w_s = jnp.tile(sgu_w[:, :, :mix_s, :mix_s], (1, 1, chunk_s // mix_s, chunk_s // mix_s))
    sgu_b_s = jnp.tile(_head_bias(sgu_b[:, :, :mix_s]), (1, chunk_s // mix_s, 1))

    tab_p = _rope_tables(jnp.arange(seq))
    tab_s = _rope_tables(past_len + (jnp.arange(rows_s) % dec))

    page_major = lambda a: jnp.transpose(a, (0, 1, 3, 4, 2)).reshape(a.shape[:2] + (-1, a.shape[2]))
    cache_kt, cache_vt = page_major(cache_moba_k), page_major(cache_moba_v)
    cache_krt = jnp.swapaxes(cache_mla_krope, 2, 3)
    hist_pad = -(-(POOL_HIST + dec) // 8) * 8

    xp = x_prompt.reshape(rows_p, d)
    xs = x_sample.reshape(rows_s, d)
    pe_p = p_prompt.reshape(depth, rows_p, -1)
    pe_s = p_sample.reshape(depth, rows_s, -1)
    st_p = [[] for _ in range(5)]
    st_s = [[] for _ in range(6)]
    for i in range(depth):
        xp = _ffn(xp, i, *f1)
        (qmla, kmla, lat, latb, kr, zb, u, v, qd, kd, kdb, vd, vdb) = _inproj(xp, i, wp, tab_p)
        out_a = _mla_prompt(qmla, kmla, latb, i, wp['w_uv'], batch=batch, seq=seq)
        out_b = _pool(zb, i, pool_w_bd, pool_sc, seq=seq, n_seq_blk=1, pos_start=0)
        out_c = _sgu(u, v, i, sgu_w_p, sgu_b_p, mix_len=mix_p, chunk=mix_p, tm=_row_tile(seq, 1024))
        out_d = _moba_prompt(qd, kdb, vdb, batch=batch, seq=seq)
        xp = _outproj(xp, out_a, out_b, out_c, out_d, i, w_out_b)
        xp = _ffn(xp, i, *f2)
        xp = _ple(xp, pe_p, i, ple_gn, ple_wg_b, ple_w_b)
        st_p[0].append(lat.reshape(batch, seq, -1))
        st_p[1].append(kr.reshape(batch, seq, -1))
        st_p[2].append(kd.reshape(batch, seq, N_HEADS, HEAD_DIM))
        st_p[3].append(vd.reshape(batch, seq, N_HEADS, HEAD_DIM))
        st_p[4].append(zb.reshape(batch, seq, -1)[:, seq - POOL_HIST:])

        xs = _ffn(xs, i, *f1)
        (qmla, kmla, lat, latb, kr, zb, u, v, qd, kd, kdb, vd, vdb) = _inproj(xs, i, wp, tab_s)
        q8 = _pad_queries(qmla, db, dec)
        qr8 = q8.reshape(db, Q_PAD, N_HEADS, MLA_HEAD_PAD)[..., MLA_NOPE:MLA_NOPE + MLA_ROPE]
        qr8 = jnp.swapaxes(qr8, 1, 2).reshape(db, N_HEADS * Q_PAD, MLA_ROPE)
        qn8 = q8.reshape(db, Q_PAD, N_HEADS, MLA_HEAD_PAD)[..., :MLA_NOPE].reshape(db, Q_PAD, -1)
        out_a = _mla_sample(page_table, q8, qn8, qr8, _pad_queries(kmla, db, dec), _pad_queries(latb, db, dec),
                            cache_mla_latent, cache_krt, i, wp, dec=dec)
        out_a = out_a[:, :dec].reshape(rows_s, -1)
        full = jnp.concatenate([state_pool[i], zb.reshape(db, dec, -1)], axis=1)
        full_p = jnp.pad(full, ((0, 0), (0, hist_pad - full.shape[1]), (0, 0)))
        out_b = _pool(full_p.reshape(db * hist_pad, -1), i, pool_w_bd, pool_sc, seq=hist_pad, n_seq_blk=db,
                      pos_start=past_len - POOL_HIST)
        out_b = out_b.reshape(db, hist_pad, -1)[:, POOL_HIST:POOL_HIST + dec].reshape(rows_s, -1)
        out_c = _sgu(u, v, i, sgu_w_s, sgu_b_s, mix_len=mix_s, chunk=chunk_s, tm=rows_s)
        out_d = _moba_sample(page_table, _pad_queries(qd, db, dec), _pad_queries(kdb, db, dec),
                             _pad_queries(vdb, db, dec), cache_kt, cache_vt, i, dec=dec)
        out_d = out_d[:, :dec].reshape(rows_s, -1)
        xs = _outproj(xs, out_a, out_b, out_c, out_d, i, w_out_b)
        xs = _ffn(xs, i, *f2)
        xs = _ple(xs, pe_s, i, ple_gn, ple_wg_b, ple_w_b)
        st_s[0].append(lat.reshape(db, dec, -1))
        st_s[1].append(kr.reshape(db, dec, -1))
        st_s[2].append(kd.reshape(db, dec, N_HEADS, HEAD_DIM))
        st_s[3].append(vd.reshape(db, dec, N_HEADS, HEAD_DIM))
        st_s[4].append(full[:, full.shape[1] - POOL_HIST:])
        st_s[5].append(v.reshape(db, dec, -1))

    sp = [jnp.stack(s) for s in st_p]
    ss = [jnp.stack(s) for s in st_s]
    return (xp.reshape(batch, seq, d), xs.reshape(db, dec, d), sp[0], sp[1], ss[0], ss[1],
            sp[2], sp[3], ss[2], ss[3], sp[4], ss[4], ss[5])
```
